```python
import math
import jax, jax.numpy as jnp
from jax import lax
import numpy as np

D_MODEL = 1024
BATCH = 16
SEQ = 2048
DEPTH = 2

CHUNK = 64
M_HEADS = 8
M_DQK = 64
M_DV = 128
M_QK = M_HEADS * M_DQK
M_V = M_HEADS * M_DV
M_CONV = 4
S_HEADDIM = 64
S_INNER = D_MODEL
S_HEADS = S_INNER // S_HEADDIM
S_GROUPS = 4
S_STATE = 128
S_CONV = 4
S_XBC = S_INNER + 2 * S_GROUPS * S_STATE
D_FF = 2752
FFN_CONV = 3
IN_SIZES = (M_QK, M_QK, M_V, M_V, M_HEADS, M_HEADS, S_INNER, S_XBC, S_HEADS, D_MODEL, D_MODEL)
D_IN = sum(IN_SIZES)
ALPHA = (2 * DEPTH) ** 0.25
BETA = (8 * DEPTH) ** -0.25
LN_EPS = 1e-5
RMS_EPS = 1e-6
NEG_BIG = -1e30

kernel_name = "hybrid_mlstm_mamba2_convffn_deepnorm"


def layer_norm(x, g, b):
    xf = x.astype(jnp.float32)
    mu = jnp.mean(xf, -1, keepdims=True)
    var = jnp.mean(jnp.square(xf - mu), -1, keepdims=True)
    return ((xf - mu) * lax.rsqrt(var + LN_EPS) * g.astype(jnp.float32) + b.astype(jnp.float32)).astype(x.dtype)


def group_rms_norm(x, g, n_groups):
    shp = x.shape
    xf = x.astype(jnp.float32).reshape(shp[:-1] + (n_groups, shp[-1] // n_groups))
    xf = xf * lax.rsqrt(jnp.mean(xf * xf, -1, keepdims=True) + RMS_EPS)
    return (xf.reshape(shp) * g.astype(jnp.float32)).astype(x.dtype)


def causal_dwconv(x, w, b):
    K, C = w.shape
    y = lax.conv_general_dilated(x, w[:, None, :], window_strides=(1,), padding=[(K - 1, 0)],
                                 dimension_numbers=('NWC', 'WIO', 'NWC'), feature_group_count=C)
    return y + b


def mlstm_chunkwise(q, k, v, i_pre, f_pre):
    f32 = jnp.float32
    Bsz, T = q.shape[0], q.shape[1]
    nc = T // CHUNK

    def chunks(a):
        a = a.reshape((Bsz, nc, CHUNK) + a.shape[2:])
        return jnp.moveaxis(a, 3, 1)

    qc = chunks(q.astype(f32))
    kc = chunks(k.astype(f32)) * (M_DQK ** -0.5)
    vc = chunks(v.astype(f32))
    log_i = chunks(i_pre.astype(f32))
    log_f = jax.nn.log_sigmoid(chunks(f_pre.astype(f32)))
    b = jnp.cumsum(log_f, -1)
    g = b[..., -1]

    a = g[..., None] - b + log_i
    m_loc = jnp.max(a, -1)
    w_s = jnp.exp(a - m_loc[..., None])
    kw = kc * w_s[..., None]
    c_loc = jnp.einsum('bhcsk,bhcsv->bhckv', kw, vc)
    n_loc = jnp.sum(kw, 3)

    def step(carry, inp):
        C, n, m = carry
        g_c, m_l, C_l, n_l = inp
        m_new = jnp.maximum(g_c + m, m_l)
        s_prev = jnp.exp(g_c + m - m_new)
        s_loc = jnp.exp(m_l - m_new)
        C_new = s_prev[..., None, None] * C + s_loc[..., None, None] * C_l
        n_new = s_prev[..., None] * n + s_loc[..., None] * n_l
        return (C_new, n_new, m_new), (C, n, m)

    init = (jnp.zeros((Bsz, M_HEADS, M_DQK, M_DV), f32),
            jnp.zeros((Bsz, M_HEADS, M_DQK), f32),
            jnp.full((Bsz, M_HEADS), NEG_BIG, f32))
    xs = (jnp.moveaxis(g, 2, 0), jnp.moveaxis(m_loc, 2, 0),
          jnp.moveaxis(c_loc, 2, 0), jnp.moveaxis(n_loc, 2, 0))
    _, (C_prev, n_prev, m_prev) = lax.scan(step, init, xs)
    C_prev = jnp.moveaxis(C_prev, 0, 2)
    n_prev = jnp.moveaxis(n_prev, 0, 2)
    m_prev = jnp.moveaxis(m_prev, 0, 2)

    causal = jnp.tril(jnp.ones((CHUNK, CHUNK), bool))
    log_d = jnp.where(causal, b[..., :, None] - b[..., None, :] + log_i[..., None, :], -jnp.inf)
    m_inter = b + m_prev[..., None]
    m_comb = jnp.maximum(m_inter, jnp.max(log_d, -1))
    s_inter = jnp.exp(m_inter - m_comb)
    scores = jnp.einsum('bhctk,bhcsk->bhcts', qc, kc) * jnp.exp(log_d - m_comb[..., None])
    num = (jnp.einsum('bhcts,bhcsv->bhctv', scores, vc)
           + s_inter[..., None] * jnp.einsum('bhctk,bhckv->bhctv', qc, C_prev))
    den = jnp.sum(scores, -1) + s_inter * jnp.einsum('bhctk,bhck->bhct', qc, n_prev)
    h = num / jnp.maximum(jnp.abs(den), jnp.exp(-m_comb))[..., None]
    h = jnp.moveaxis(h, 1, 3).reshape(Bsz, T, M_HEADS, M_DV)
    return h.astype(q.dtype)


def segsum_exp(a):
    cs = jnp.cumsum(a, -1)
    L = a.shape[-1]
    causal = jnp.tril(jnp.ones((L, L), bool))
    return jnp.exp(jnp.where(causal, cs[..., :, None] - cs[..., None, :], -jnp.inf))


def ssd_chunked(x, dt, A, Bm, Cm):
    f32 = jnp.float32
    Bsz, T = x.shape[0], x.shape[1]
    nc = T // CHUNK
    R = S_HEADS // S_GROUPS
    dt = dt.astype(f32)
    xc = (x.astype(f32) * dt[..., None]).reshape(Bsz, nc, CHUNK, S_GROUPS, R, S_HEADDIM)
    a = (dt * A.astype(f32)).reshape(Bsz, nc, CHUNK, S_GROUPS, R)
    a = jnp.transpose(a, (0, 3, 4, 1, 2))
    Bc = Bm.astype(f32).reshape(Bsz, nc, CHUNK, S_GROUPS, S_STATE)
    Cc = Cm.astype(f32).reshape(Bsz, nc, CHUNK, S_GROUPS, S_STATE)
    a_cs = jnp.cumsum(a, -1)

    cb = jnp.einsum('bclgn,bcsgn->bgcls', Cc, Bc)
    mix = cb[:, :, None] * segsum_exp(a)
    y_diag = jnp.einsum('bgrcls,bcsgrp->bclgrp', mix, xc)

    decay_to_end = jnp.transpose(jnp.exp(a_cs[..., -1:] - a_cs), (0, 3, 4, 1, 2))
    states = jnp.einsum('bcsgn,bcsgrp->bcgrpn', Bc, xc * decay_to_end[..., None])

    chunk_decay = jnp.exp(a_cs[..., -1])

    def step(h, inp):
        dec, st = inp
        return dec[..., None, None] * h + st, h

    h0 = jnp.zeros((Bsz, S_GROUPS, R, S_HEADDIM, S_STATE), f32)
    _, h_prev = lax.scan(step, h0, (jnp.moveaxis(chunk_decay, 3, 0), jnp.moveaxis(states, 1, 0)))
    h_prev = jnp.moveaxis(h_prev, 0, 1)

    decay_in = jnp.transpose(jnp.exp(a_cs), (0, 3, 4, 1, 2))
    y_off = jnp.einsum('bclgn,bcgrpn->bclgrp', Cc, h_prev) * decay_in[..., None]
    return (y_diag + y_off).reshape(Bsz, T, S_HEADS, S_HEADDIM)


def mlstm_branch(q, k, v, o_pre, i_pre, f_pre, conv_w, conv_b, i_bias, f_bias, norm_g):
    Bsz, T = q.shape[0], q.shape[1]
    qk = jax.nn.silu(causal_dwconv(jnp.concatenate([q, k], -1), conv_w, conv_b))
    q, k = jnp.split(qk, 2, -1)
    h = mlstm_chunkwise(q.reshape(Bsz, T, M_HEADS, M_DQK), k.reshape(Bsz, T, M_HEADS, M_DQK),
                        v.reshape(Bsz, T, M_HEADS, M_DV), i_pre + i_bias, f_pre + f_bias)
    h = group_rms_norm(h.reshape(Bsz, T, M_V), norm_g, M_HEADS)
    return jax.nn.sigmoid(o_pre) * h


def mamba2_branch(z, xbc, dt_raw, conv_w, conv_b, dt_bias, a_log, d_skip, norm_g):
    Bsz, T = z.shape[0], z.shape[1]
    xbc = jax.nn.silu(causal_dwconv(xbc, conv_w, conv_b))
    xs, Bm, Cm = jnp.split(xbc, [S_INNER, S_INNER + S_GROUPS * S_STATE], -1)
    xs = xs.reshape(Bsz, T, S_HEADS, S_HEADDIM)
    Bm = Bm.reshape(Bsz, T, S_GROUPS, S_STATE)
    Cm = Cm.reshape(Bsz, T, S_GROUPS, S_STATE)
    dt = jax.nn.softplus(dt_raw.astype(jnp.float32) + dt_bias.astype(jnp.float32))
    A = -jnp.exp(a_log.astype(jnp.float32))
    y = ssd_chunked(xs, dt, A, Bm, Cm) + d_skip.astype(jnp.float32)[:, None] * xs.astype(jnp.float32)
    y = y.reshape(Bsz, T, S_INNER).astype(z.dtype)
    return group_rms_norm(y * jax.nn.silu(z), norm_g, S_GROUPS)


def setup_inputs(seed: int = 0) -> dict:
    key = jax.random.key(seed)
    ks = jax.random.split(key, 32)
    L = DEPTH
    f32 = jnp.float32

    def nrm(k, shape, scale):
        return scale * jax.random.normal(k, shape, f32)

    dt0 = jnp.exp(jax.random.uniform(ks[12], (L, S_HEADS), f32, math.log(1e-3), math.log(1e-1)))
    return {
        "x": nrm(ks[0], (BATCH, SEQ, D_MODEL), 1.0),
        "in_ln_g": 1.0 + nrm(ks[1], (D_MODEL,), 0.02),
        "in_ln_b": nrm(ks[2], (D_MODEL,), 0.02),
        "w_in": nrm(ks[3], (L, D_MODEL, D_IN), D_MODEL ** -0.5),
        "m_conv_w": nrm(ks[4], (L, M_CONV, 2 * M_QK), M_CONV ** -0.5),
        "m_conv_b": nrm(ks[5], (L, 2 * M_QK), 0.01),
        "m_i_bias": nrm(ks[6], (L, M_HEADS), 0.1),
        "m_f_bias": jnp.linspace(3.0, 6.0, M_HEADS, dtype=f32)[None] + nrm(ks[7], (L, M_HEADS), 0.1),
        "m_norm_g": 1.0 + nrm(ks[8], (L, M_V), 0.02),
        "s_conv_w": nrm(ks[9], (L, S_CONV, S_XBC), S_CONV ** -0.5),
        "s_conv_b": nrm(ks[10], (L, S_XBC), 0.01),
        "s_dt_bias": dt0 + jnp.log(-jnp.expm1(-dt0)),
        "s_a_log": jnp.log(jax.random.uniform(ks[13], (L, S_HEADS), f32, 1.0, 16.0)),
        "s_d": 1.0 + nrm(ks[14], (L, S_HEADS), 0.1),
        "s_norm_g": 1.0 + nrm(ks[15], (L, S_INNER), 0.02),
        "p_a": nrm(ks[16], (L, M_V, D_MODEL), M_V ** -0.5),
        "p_b": nrm(ks[17], (L, S_INNER, D_MODEL), S_INNER ** -0.5),
        "w_out": nrm(ks[18], (L, D_MODEL, D_MODEL), BETA * D_MODEL ** -0.5),
        "ln1_g": 1.0 + nrm(ks[19], (L, D_MODEL), 0.02),
        "ln1_b": nrm(ks[20], (L, D_MODEL), 0.02),
        "w_up": nrm(ks[21], (L, D_MODEL, 2 * D_FF), D_MODEL ** -0.5),
        "f_conv_w": nrm(ks[22], (L, FFN_CONV, 2 * D_FF), FFN_CONV ** -0.5),
        "f_conv_b": nrm(ks[23], (L, 2 * D_FF), 0.01),
        "w_down": nrm(ks[24], (L, D_FF, D_MODEL), BETA * D_FF ** -0.5),
        "ln2_g": 1.0 + nrm(ks[25], (L, D_MODEL), 0.02),
        "ln2_b": nrm(ks[26], (L, D_MODEL), 0.02),
    }


def reference(x, in_ln_g, in_ln_b, w_in, m_conv_w, m_conv_b, m_i_bias, m_f_bias, m_norm_g,
              s_conv_w, s_conv_b, s_dt_bias, s_a_log, s_d, s_norm_g, p_a, p_b, w_out,
              ln1_g, ln1_b, w_up, f_conv_w, f_conv_b, w_down, ln2_g, ln2_b):
    splits = np.cumsum(IN_SIZES)[:-1].tolist()
    x = layer_norm(x, in_ln_g, in_ln_b)
    for l in range(DEPTH):
        proj = x @ w_in[l]
        q, k, v, o_pre, i_pre, f_pre, z, xbc, dt_raw, g_a, g_b = jnp.split(proj, splits, -1)
        h_a = mlstm_branch(q, k, v, o_pre, i_pre, f_pre, m_conv_w[l], m_conv_b[l],
                           m_i_bias[l], m_f_bias[l], m_norm_g[l])
        h_b = mamba2_branch(z, xbc, dt_raw, s_conv_w[l], s_conv_b[l], s_dt_bias[l],
                            s_a_log[l], s_d[l], s_norm_g[l])
        merged = jax.nn.sigmoid(g_a) * (h_a @ p_a[l]) + jax.nn.sigmoid(g_b) * (h_b @ p_b[l])
        x = layer_norm(ALPHA * x + merged @ w_out[l], ln1_g[l], ln1_b[l])
        u = causal_dwconv(x @ w_up[l], f_conv_w[l], f_conv_b[l])
        u_gate, u_val = jnp.split(u, 2, -1)
        x = layer_norm(ALPHA * x + (jax.nn.silu(u_gate) * u_val) @ w_down[l], ln2_g[l], ln2_b[l])
    return x
```

```python
import functools

import jax
import jax.numpy as jnp
from jax import lax
from jax.experimental import pallas as pl
from jax.experimental.pallas import tpu as pltpu

F32 = jnp.float32
BF16 = jnp.bfloat16

D_MODEL = 1024
M_HEADS = 8
M_DQK = 64
M_DV = 128
M_QK = M_HEADS * M_DQK
M_V = M_HEADS * M_DV
M_CONV = 4
S_HEADDIM = 64
S_INNER = D_MODEL
S_HEADS = S_INNER // S_HEADDIM
S_GROUPS = 4
S_STATE = 128
S_CONV = 4
S_BC = S_GROUPS * S_STATE
S_XBC = S_INNER + 2 * S_BC
D_FF = 2752
FFN_CONV = 3
IN_SIZES = (M_QK, M_QK, M_V, M_V, M_HEADS, M_HEADS, S_INNER, S_XBC, S_HEADS, D_MODEL, D_MODEL)
LN_EPS = 1e-5
RMS_EPS = 1e-6
NEG_BIG = -1e30

LANES = 128
SUBLANES = 8
D_FF_PAD = 2816
SCAN_CHUNK = 256
FFN_TILE = 512
FFN_COLS = 256
ROW_TILE = 1024
PROJ_COLS = 1024
VMEM_LIMIT = 56 * 1024 * 1024

OFF_Q = 0
OFF_K = OFF_Q + M_QK
OFF_V = OFF_K + M_QK
OFF_O = OFF_V + M_V
OFF_Z = OFF_O + M_V
OFF_XBC = OFF_Z + S_INNER
OFF_GA = OFF_XBC + S_XBC
OFF_GB = OFF_GA + D_MODEL
BIG_COLS = OFF_GB + D_MODEL
LANE_I = 0
LANE_F = LANE_I + M_HEADS
LANE_DT = LANE_F + M_HEADS


def _layer_norm(y, g, b):
    mu = jnp.mean(y, axis=-1, keepdims=True)
    yc = y - mu
    var = jnp.mean(yc * yc, axis=-1, keepdims=True)
    return yc * lax.rsqrt(var + LN_EPS) * g + b


def _silu(x):
    return x * jax.nn.sigmoid(x)


def _causal_conv(cur, prev8, w_ref, b_ref, width):
    row8 = lax.broadcasted_iota(jnp.int32, prev8.shape, 0)
    acc = cur * w_ref[width - 1:width, :] + b_ref[...]
    for k in range(1, width):
        rolled = pltpu.roll(cur, k, axis=0)
        head = jnp.where(row8 < k, pltpu.roll(prev8, k, axis=0), rolled[:SUBLANES])
        shifted = jnp.concatenate([head, rolled[SUBLANES:]], axis=0)
        acc = acc + shifted * w_ref[width - 1 - k:width - k, :]
    return acc


def _ln_kernel(x_ref, g_ref, b_ref, o_ref):
    o_ref[...] = _layer_norm(x_ref[...], g_ref[...], b_ref[...])


def _entry_ln(x2, g, b):
    n = x2.shape[0]
    return pl.pallas_call(
        _ln_kernel,
        grid=(n // ROW_TILE,),
        in_specs=[pl.BlockSpec((ROW_TILE, D_MODEL), lambda i: (i, 0)),
                  pl.BlockSpec((1, D_MODEL), lambda i: (0, 0)),
                  pl.BlockSpec((1, D_MODEL), lambda i: (0, 0))],
        out_specs=pl.BlockSpec((ROW_TILE, D_MODEL), lambda i: (i, 0)),
        out_shape=jax.ShapeDtypeStruct((n, D_MODEL), F32),
        compiler_params=pltpu.CompilerParams(dimension_semantics=("parallel",)),
        name="entry_ln",
    )(x2, g, b)


def _proj_kernel(x_ref, wb_ref, ws_ref, ob_ref, os_ref, xb_ref):
    @pl.when(pl.program_id(1) == 0)
    def _():
        xb = x_ref[...].astype(BF16)
        xb_ref[...] = xb
        os_ref[...] = jnp.dot(xb, ws_ref[...], preferred_element_type=F32)

    ob_ref[...] = jnp.dot(xb_ref[...], wb_ref[...], preferred_element_type=F32).astype(BF16)


def _proj(x2, w_big, w_small):
    n = x2.shape[0]
    return pl.pallas_call(
        _proj_kernel,
        grid=(n // ROW_TILE, BIG_COLS // PROJ_COLS),
        in_specs=[pl.BlockSpec((ROW_TILE, D_MODEL), lambda i, j: (i, 0)),
                  pl.BlockSpec((D_MODEL, PROJ_COLS), lambda i, j: (0, j)),
                  pl.BlockSpec((D_MODEL, LANES), lambda i, j: (0, 0))],
        out_specs=[pl.BlockSpec((ROW_TILE, PROJ_COLS), lambda i, j: (i, j)),
                   pl.BlockSpec((ROW_TILE, LANES), lambda i, j: (i, 0))],
        out_shape=[jax.ShapeDtypeStruct((n, BIG_COLS), BF16),
                   jax.ShapeDtypeStruct((n, LANES), F32)],
        scratch_shapes=[pltpu.VMEM((ROW_TILE, D_MODEL), BF16)],
        compiler_params=pltpu.CompilerParams(
            dimension_semantics=("parallel", "arbitrary"), vmem_limit_bytes=VMEM_LIMIT),
        name="in_proj",
    )(x2, w_big, w_small)


def _mixer_kernel(alpha, x_ref, pb_ref, ps_ref, mcw_ref, mcb_ref, scw_ref, scb_ref, gbias_ref, aneg_ref,
                  dskip_ref, mng_ref, sng_ref, pa_ref, pbw_ref, wo_ref, lng_ref, lnb_ref, o_ref,
                  mcar_ref, scar_ref, mst_ref, mm_ref, sst_ref, ha_ref, hb_ref):
    L = SCAN_CHUNK
    neg_inf = F32(-jnp.inf)

    @pl.when(pl.program_id(1) == 0)
    def _():
        mcar_ref[...] = jnp.zeros_like(mcar_ref)
        scar_ref[...] = jnp.zeros_like(scar_ref)
        mst_ref[...] = jnp.zeros_like(mst_ref)
        sst_ref[...] = jnp.zeros_like(sst_ref)
        mm_ref[...] = jnp.full_like(mm_ref, NEG_BIG)

    ps = ps_ref[...] + gbias_ref[...]
    lane = lax.broadcasted_iota(jnp.int32, (L, LANES), 1)
    tail = jnp.log1p(jnp.exp(-jnp.abs(ps)))
    log_sig = jnp.minimum(ps, 0.0) - tail
    soft_plus = jnp.maximum(ps, 0.0) + tail
    is_f = (lane >= LANE_F) & (lane < LANE_DT)
    to_cum = jnp.where(is_f, log_sig, soft_plus * aneg_ref[...])
    rows = lax.broadcasted_iota(jnp.int32, (L, L), 0)
    cols = lax.broadcasted_iota(jnp.int32, (L, L), 1)
    causal = rows >= cols
    tril = jnp.where(causal, 1.0, 0.0).astype(F32)
    cum = jnp.dot(tril, to_cum, precision=lax.Precision.HIGHEST, preferred_element_type=F32)
    packed = jnp.where(lane < LANE_F, ps, cum)
    packed_t = packed.T

    qk_raw = pb_ref[:, OFF_Q:OFF_V].astype(F32)
    qk = _silu(_causal_conv(qk_raw, mcar_ref[...], mcw_ref, mcb_ref, M_CONV))
    mcar_ref[...] = qk_raw[L - SUBLANES:, :]
    q_all = qk[:, :M_QK]
    k_all = qk[:, M_QK:] * (M_DQK ** -0.5)
    k_t = k_all.T
    lane_pair = lax.broadcasted_iota(jnp.int32, (L, 2 * M_DQK), 1)
    ones_col = jnp.where(lane_pair == 0, 1.0, 0.0).astype(BF16)

    for h in range(M_HEADS):
        pair, half = divmod(h, 2)
        li_col = packed[:, LANE_I + h:LANE_I + h + 1]
        b_col = packed[:, LANE_F + h:LANE_F + h + 1]
        w_col = li_col - b_col
        w_row = packed_t[LANE_I + h:LANE_I + h + 1, :] - packed_t[LANE_F + h:LANE_F + h + 1, :]
        w_mat = jnp.where(causal, w_row, neg_inf)
        m_prev = mm_ref[h:h + 1, 0:1]
        mm = jnp.maximum(jnp.max(w_mat, axis=1, keepdims=True), m_prev)
        decay = jnp.exp(w_mat - mm)
        s_inter = jnp.exp(m_prev - mm)
        in_head = (lane_pair >= half * M_DQK) & (lane_pair < (half + 1) * M_DQK)
        q_pair = q_all[:, pair * 2 * M_DQK:(pair + 1) * 2 * M_DQK]
        k_pair = k_all[:, pair * 2 * M_DQK:(pair + 1) * 2 * M_DQK]
        q_h = jnp.where(in_head, q_pair, 0.0).astype(BF16)
        scores = lax.dot_general(q_h, k_pair.astype(BF16), (((1,), (1,)), ((), ())),
                                 preferred_element_type=F32) * decay
        v_ext = jnp.concatenate([pb_ref[:, OFF_V + h * M_DV:OFF_V + (h + 1) * M_DV], ones_col], axis=1)
        state_pair = mst_ref[pair]
        num_den = (jnp.dot(scores.astype(BF16), v_ext, preferred_element_type=F32)
                   + s_inter * jnp.dot(q_h, state_pair.astype(BF16), preferred_element_type=F32))
        num = num_den[:, :M_DV]
        den = num_den[:, M_DV:M_DV + 1]
        h_out = num / jnp.maximum(jnp.abs(den), jnp.exp(-(b_col + mm)))
        mm_last = mm[L - 1:L, :]
        kw_t = (k_t[h * M_DQK:(h + 1) * M_DQK, :] * jnp.exp(w_row - mm_last)).astype(BF16)
        state_h = mst_ref[pair, half * M_DQK:(half + 1) * M_DQK, :]
        mst_ref[pair, half * M_DQK:(half + 1) * M_DQK, :] = (
            jnp.exp(m_prev - mm_last) * state_h + jnp.dot(kw_t, v_ext, preferred_element_type=F32))
        mm_ref[h:h + 1, :] = jnp.broadcast_to(b_col[L - 1:L, :] + mm_last, (1, LANES))
        ms = jnp.mean(h_out * h_out, axis=1, keepdims=True)
        h_n = h_out * lax.rsqrt(ms + RMS_EPS) * mng_ref[:, h * M_DV:(h + 1) * M_DV]
        o_gate = jax.nn.sigmoid(pb_ref[:, OFF_O + h * M_DV:OFF_O + (h + 1) * M_DV].astype(F32))
        ha_ref[:, h * M_DV:(h + 1) * M_DV] = (o_gate * h_n).astype(BF16)

    xbc_raw = pb_ref[:, OFF_XBC:OFF_XBC + S_XBC].astype(F32)
    xbc = _silu(_causal_conv(xbc_raw, scar_ref[...], scw_ref, scb_ref, S_CONV))
    scar_ref[...] = xbc_raw[L - SUBLANES:, :]
    b_t = xbc[:, S_INNER:S_INNER + S_BC].T
    gw = S_INNER // S_GROUPS
    hpg = S_HEADS // S_GROUPS
    lane_g = lax.broadcasted_iota(jnp.int32, (L, gw), 1)
    lane_g_row = lax.broadcasted_iota(jnp.int32, (1, gw), 1)

    for g in range(S_GROUPS):
        xs_g = xbc[:, g * gw:(g + 1) * gw]
        bm_g = xbc[:, S_INNER + g * S_STATE:S_INNER + (g + 1) * S_STATE].astype(BF16)
        cm_g = xbc[:, S_INNER + S_BC + g * S_STATE:S_INNER + S_BC + (g + 1) * S_STATE].astype(BF16)
        cb = lax.dot_general(cm_g, bm_g, (((1,), (1,)), ((), ())), preferred_element_type=F32)
        y_off = jnp.dot(cm_g, sst_ref[g].astype(BF16), preferred_element_type=F32)
        dt_g = jnp.zeros((L, gw), F32)
        acs_g = jnp.zeros((L, gw), F32)
        last_g = jnp.zeros((1, gw), F32)
        acs_cols, acs_rows = [], []
        for r in range(hpg):
            hh = g * hpg + r
            acs_col = packed[:, LANE_DT + hh:LANE_DT + hh + 1]
            acs_cols.append(acs_col)
            acs_rows.append(packed_t[LANE_DT + hh:LANE_DT + hh + 1, :])
            sel = (lane_g >= r * S_HEADDIM) & (lane_g < (r + 1) * S_HEADDIM)
            dt_g = jnp.where(sel, soft_plus[:, LANE_DT + hh:LANE_DT + hh + 1], dt_g)
            acs_g = jnp.where(sel, acs_col, acs_g)
            sel_row = (lane_g_row >= r * S_HEADDIM) & (lane_g_row < (r + 1) * S_HEADDIM)
            last_g = jnp.where(sel_row, acs_col[L - 1:L, :], last_g)
        xdt_g = xs_g * dt_g
        xdt_b = xdt_g.astype(BF16)
        y_diag = jnp.zeros((L, gw), F32)
        for r in range(hpg):
            seg = jnp.exp(jnp.where(causal, acs_cols[r] - acs_rows[r], neg_inf))
            mix = (cb * seg).astype(BF16)
            sel = (lane_g >= r * S_HEADDIM) & (lane_g < (r + 1) * S_HEADDIM)
            y_diag = y_diag + jnp.dot(mix, jnp.where(sel, xdt_b, jnp.zeros_like(xdt_b)),
                                      preferred_element_type=F32)
        y_g = y_diag + y_off * jnp.exp(acs_g) + dskip_ref[:, g * gw:(g + 1) * gw] * xs_g
        x_dec = (xdt_g * jnp.exp(last_g - acs_g)).astype(BF16)
        sst_ref[g] = (sst_ref[g] * jnp.exp(last_g)
                      + jnp.dot(b_t[g * S_STATE:(g + 1) * S_STATE, :].astype(BF16), x_dec,
                                preferred_element_type=F32))
        gz = y_g * _silu(pb_ref[:, OFF_Z + g * gw:OFF_Z + (g + 1) * gw].astype(F32))
        ms = jnp.mean(gz * gz, axis=1, keepdims=True)
        hb_ref[:, g * gw:(g + 1) * gw] = (gz * lax.rsqrt(ms + RMS_EPS)
                                          * sng_ref[:, g * gw:(g + 1) * gw]).astype(BF16)

    gate_a = jax.nn.sigmoid(pb_ref[:, OFF_GA:OFF_GA + D_MODEL].astype(F32))
    gate_b = jax.nn.sigmoid(pb_ref[:, OFF_GB:OFF_GB + D_MODEL].astype(F32))
    merged = (gate_a * jnp.dot(ha_ref[...], pa_ref[...], preferred_element_type=F32)
              + gate_b * jnp.dot(hb_ref[...], pbw_ref[...], preferred_element_type=F32))
    y = alpha * x_ref[...] + jnp.dot(merged.astype(BF16), wo_ref[...], preferred_element_type=F32)
    o_ref[...] = _layer_norm(y, lng_ref[...], lnb_ref[...])


def _mixer(alpha, x2, p_big, p_small, prm, batch, seq):
    L = SCAN_CHUNK
    nt = seq // L
    row = lambda b, t: (b * nt + t, 0)
    const = lambda b, t: (0, 0)

    def full(a):
        return pl.BlockSpec(a.shape, const)

    args = (x2, p_big, p_small, prm["mcw"], prm["mcb"], prm["scw"], prm["scb"], prm["gbias"], prm["aneg"],
            prm["dskip"], prm["mng"], prm["sng"], prm["pa"], prm["pb"], prm["wo"], prm["ln1g"], prm["ln1b"])
    in_specs = [pl.BlockSpec((L, D_MODEL), row), pl.BlockSpec((L, BIG_COLS), row), pl.BlockSpec((L, LANES), row)]
    in_specs += [full(a) for a in args[3:]]
    return pl.pallas_call(
        functools.partial(_mixer_kernel, alpha),
        grid=(batch, nt),
        in_specs=in_specs,
        out_specs=pl.BlockSpec((L, D_MODEL), row),
        out_shape=jax.ShapeDtypeStruct((batch * seq, D_MODEL), F32),
        scratch_shapes=[
            pltpu.VMEM((SUBLANES, 2 * M_QK), F32),
            pltpu.VMEM((SUBLANES, S_XBC), F32),
            pltpu.VMEM((M_HEADS // 2, 2 * M_DQK, 2 * M_DV), F32),
            pltpu.VMEM((M_HEADS, LANES), F32),
            pltpu.VMEM((S_GROUPS, S_STATE, S_INNER // S_GROUPS), F32),
            pltpu.VMEM((L, M_V), BF16),
            pltpu.VMEM((L, S_INNER), BF16),
        ],
        compiler_params=pltpu.CompilerParams(
            dimension_semantics=("parallel", "arbitrary"), vmem_limit_bytes=VMEM_LIMIT),
        name="mixer",
    )(*args)


def _ffn_kernel(alpha, x_ref, wg_ref, wv_ref, cwg_ref, cwv_ref, cbg_ref, cbv_ref, wd_ref, lng_ref, lnb_ref,
                o_ref, car_g_ref, car_v_ref, acc_ref):
    tt = FFN_TILE

    @pl.when(pl.program_id(1) == 0)
    def _():
        car_g_ref[...] = jnp.zeros_like(car_g_ref)
        car_v_ref[...] = jnp.zeros_like(car_v_ref)

    xb = x_ref[...].astype(BF16)
    for j in range(D_FF_PAD // FFN_COLS):
        c0, c1 = j * FFN_COLS, (j + 1) * FFN_COLS
        ug = jnp.dot(xb, wg_ref[:, c0:c1], preferred_element_type=F32)
        uv = jnp.dot(xb, wv_ref[:, c0:c1], preferred_element_type=F32)
        cg = _causal_conv(ug, car_g_ref[:, c0:c1], cwg_ref.at[:, c0:c1], cbg_ref.at[:, c0:c1], FFN_CONV)
        cv = _causal_conv(uv, car_v_ref[:, c0:c1], cwv_ref.at[:, c0:c1], cbv_ref.at[:, c0:c1], FFN_CONV)
        car_g_ref[:, c0:c1] = ug[tt - SUBLANES:, :]
        car_v_ref[:, c0:c1] = uv[tt - SUBLANES:, :]
        act = (_silu(cg) * cv).astype(BF16)
        part = jnp.dot(act, wd_ref[c0:c1, :], preferred_element_type=F32)
        if j == 0:
            acc_ref[...] = part
        else:
            acc_ref[...] += part
    y = alpha * x_ref[...] + acc_ref[...]
    o_ref[...] = _layer_norm(y, lng_ref[...], lnb_ref[...])


def _ffn(alpha, x2, prm, batch, seq):
    tt = FFN_TILE
    nt = seq // tt
    row = lambda b, t: (b * nt + t, 0)
    const = lambda b, t: (0, 0)

    def full(a):
        return pl.BlockSpec(a.shape, const)

    args = (x2, prm["wg"], prm["wv"], prm["cwg"], prm["cwv"], prm["cbg"], prm["cbv"], prm["wd"],
            prm["ln2g"], prm["ln2b"])
    return pl.pallas_call(
        functools.partial(_ffn_kernel, alpha),
        grid=(batch, nt),
        in_specs=[pl.BlockSpec((tt, D_MODEL), row)] + [full(a) for a in args[1:]],
        out_specs=pl.BlockSpec((tt, D_MODEL), row),
        out_shape=jax.ShapeDtypeStruct((batch * seq, D_MODEL), F32),
        scratch_shapes=[pltpu.VMEM((SUBLANES, D_FF_PAD), F32),
                        pltpu.VMEM((SUBLANES, D_FF_PAD), F32),
                        pltpu.VMEM((tt, D_MODEL), F32)],
        compiler_params=pltpu.CompilerParams(
            dimension_semantics=("parallel", "arbitrary"), vmem_limit_bytes=VMEM_LIMIT),
        name="conv_ffn",
    )(*args)


def _layer_params(l, w_in, m_conv_w, m_conv_b, m_i_bias, m_f_bias, m_norm_g, s_conv_w, s_conv_b, s_dt_bias,
                  s_a_log, s_d, s_norm_g, p_a, p_b, w_out, ln1_g, ln1_b, w_up, f_conv_w, f_conv_b, w_down,
                  ln2_g, ln2_b):
    offs = [0]
    for s in IN_SIZES:
        offs.append(offs[-1] + s)
    w = w_in[l]
    part = [w[:, offs[i]:offs[i + 1]] for i in range(len(IN_SIZES))]
    wq, wk, wv, wo_, wi, wf, wz, wxbc, wdt, wga, wgb = part
    row = lambda a: a.reshape(1, -1).astype(F32)
    pad_small = LANES - 2 * M_HEADS - S_HEADS
    pad_ff = D_FF_PAD - D_FF
    pad_cols = lambda a: jnp.pad(a, ((0, 0), (0, pad_ff)))
    prm = {
        "w_big": jnp.concatenate([wq, wk, wv, wo_, wz, wxbc, wga, wgb], axis=1).astype(BF16),
        "w_small": jnp.pad(jnp.concatenate([wi, wf, wdt], axis=1), ((0, 0), (0, pad_small))).astype(BF16),
        "mcw": m_conv_w[l].astype(F32), "mcb": row(m_conv_b[l]),
        "scw": s_conv_w[l].astype(F32), "scb": row(s_conv_b[l]),
        "gbias": jnp.pad(jnp.concatenate([m_i_bias[l], m_f_bias[l], s_dt_bias[l]]).astype(F32),
                         (0, pad_small)).reshape(1, LANES),
        "aneg": jnp.pad(-jnp.exp(s_a_log[l].astype(F32)), (LANE_DT, pad_small)).reshape(1, LANES),
        "dskip": jnp.repeat(s_d[l].astype(F32), S_HEADDIM).reshape(1, S_INNER),
        "mng": row(m_norm_g[l]), "sng": row(s_norm_g[l]),
        "pa": p_a[l].astype(BF16), "pb": p_b[l].astype(BF16), "wo": w_out[l].astype(BF16),
        "ln1g": row(ln1_g[l]), "ln1b": row(ln1_b[l]),
        "wg": pad_cols(w_up[l][:, :D_FF]).astype(BF16), "wv": pad_cols(w_up[l][:, D_FF:]).astype(BF16),
        "cwg": pad_cols(f_conv_w[l][:, :D_FF]).astype(F32), "cwv": pad_cols(f_conv_w[l][:, D_FF:]).astype(F32),
        "cbg": pad_cols(row(f_conv_b[l][:D_FF])), "cbv": pad_cols(row(f_conv_b[l][D_FF:])),
        "wd": jnp.pad(w_down[l], ((0, pad_ff), (0, 0))).astype(BF16),
        "ln2g": row(ln2_g[l]), "ln2b": row(ln2_b[l]),
    }
    return prm


def kernel(x, in_ln_g, in_ln_b, w_in, m_conv_w, m_conv_b, m_i_bias, m_f_bias, m_norm_g, s_conv_w, s_conv_b, s_dt_bias, s_a_log, s_d, s_norm_g, p_a, p_b, w_out, ln1_g, ln1_b, w_up, f_conv_w, f_conv_b, w_down, ln2_g, ln2_b):
    batch, seq, d_model = x.shape
    depth = w_in.shape[0]
    assert d_model == D_MODEL and w_in.shape[2] == sum(IN_SIZES)
    assert seq % SCAN_CHUNK == 0 and seq % FFN_TILE == 0 and (batch * seq) % ROW_TILE == 0
    alpha = float((2 * depth) ** 0.25)
    x2 = _entry_ln(x.reshape(batch * seq, D_MODEL).astype(F32),
                   in_ln_g.reshape(1, -1).astype(F32), in_ln_b.reshape(1, -1).astype(F32))
    for l in range(depth):
        prm = _layer_params(l, w_in, m_conv_w, m_conv_b, m_i_bias, m_f_bias, m_norm_g, s_conv_w, s_conv_b,
                            s_dt_bias, s_a_log, s_d, s_norm_g, p_a, p_b, w_out, ln1_g, ln1_b, w_up, f_conv_w,
                            f_conv_b, w_down, ln2_g, ln2_b)
        p_big, p_small = _proj(x2, prm["w_big"], prm["w_small"])
        x2 = _mixer(alpha, x2, p_big, p_small, prm, batch, seq)
        x2 = _ffn(alpha, x2, prm, batch, seq)
    return x2.reshape(batch, seq, D_MODEL).astype(x.dtype)
```

```python
import functools

import jax
import jax.numpy as jnp
from jax import lax
from jax.experimental import pallas as pl
from jax.experimental.pallas import tpu as pltpu

F32 = jnp.float32
BF16 = jnp.bfloat16

D_MODEL = 1024
M_HEADS = 8
M_DQK = 64
M_DV = 128
M_QK = M_HEADS * M_DQK
M_V = M_HEADS * M_DV
M_CONV = 4
S_HEADDIM = 64
S_INNER = D_MODEL
S_HEADS = S_INNER // S_HEADDIM
S_GROUPS = 4
S_STATE = 128
S_CONV = 4
S_BC = S_GROUPS * S_STATE
S_XBC = S_INNER + 2 * S_BC
D_FF = 2752
FFN_CONV = 3
IN_SIZES = (M_QK, M_QK, M_V, M_V, M_HEADS, M_HEADS, S_INNER, S_XBC, S_HEADS, D_MODEL, D_MODEL)
LN_EPS = 1e-5
RMS_EPS = 1e-6
NEG_BIG = -1e30

LANES = 128
SUBLANES = 8
D_FF_PAD = 2816
TILE = 256
TILE_GROUPS = TILE // SUBLANES
FFN_COLS = 256
FFN_BLOCKS = D_FF_PAD // FFN_COLS
ROW_TILE = 1024
PROJ_COLS = 2048
VMEM_LIMIT = 56 * 1024 * 1024
MIX_CONV_COLS = 2 * M_QK + S_XBC
MIX_PREFIX = (M_CONV - 1) * SUBLANES
FFN_PREFIX = (FFN_CONV - 1) * SUBLANES

OFF_Q = 0
OFF_K = OFF_Q + M_QK
OFF_V = OFF_K + M_QK
OFF_O = OFF_V + M_V
OFF_Z = OFF_O + M_V
OFF_XBC = OFF_Z + S_INNER
OFF_GA = OFF_XBC + S_XBC
OFF_GB = OFF_GA + D_MODEL
BIG_COLS = OFF_GB + D_MODEL
LANE_I = 0
LANE_F = LANE_I + M_HEADS
LANE_DT = LANE_F + M_HEADS


def _layer_norm(y, g, b):
    mu = jnp.mean(y, axis=-1, keepdims=True)
    yc = y - mu
    var = jnp.mean(yc * yc, axis=-1, keepdims=True)
    return yc * lax.rsqrt(var + LN_EPS) * g + b


def _silu(x):
    return x * jax.nn.sigmoid(x)


def _token_index(shape, axis):
    p = lax.broadcasted_iota(jnp.int32, shape, axis)
    return (p & (SUBLANES - 1)) * TILE_GROUPS + (p >> 3)


def _fill_conv_prefix(stage, carry, width):
    n_pre = width - 1
    row8 = lax.broadcasted_iota(jnp.int32, (SUBLANES, stage.shape[1]), 0)
    for g in range(n_pre):
        r0 = TILE + g * SUBLANES
        cur = stage[r0:r0 + SUBLANES, :]
        prev = carry[g * SUBLANES:(g + 1) * SUBLANES, :]
        stage[g * SUBLANES:(g + 1) * SUBLANES, :] = jnp.where(
            row8 == 0, pltpu.roll(prev, 1, axis=0), pltpu.roll(cur, 1, axis=0))
        carry[g * SUBLANES:(g + 1) * SUBLANES, :] = cur


def _conv_taps(stage, w_ref, b_ref, width, c0, c1):
    p = (width - 1) * SUBLANES
    acc = b_ref[:, c0:c1] + w_ref[width - 1:width, c0:c1] * stage[p:p + TILE, c0:c1]
    for k in range(1, width):
        r0 = p - k * SUBLANES
        acc = acc + w_ref[width - 1 - k:width - k, c0:c1] * stage[r0:r0 + TILE, c0:c1]
    return acc


def _ln_kernel(x_ref, g_ref, b_ref, o_ref):
    o_ref[...] = _layer_norm(x_ref[...], g_ref[...], b_ref[...])


def _entry_ln(x2, g, b):
    n = x2.shape[0]
    return pl.pallas_call(
        _ln_kernel,
        grid=(n // ROW_TILE,),
        in_specs=[pl.BlockSpec((ROW_TILE, D_MODEL), lambda i: (i, 0)),
                  pl.BlockSpec((1, D_MODEL), lambda i: (0, 0)),
                  pl.BlockSpec((1, D_MODEL), lambda i: (0, 0))],
        out_specs=pl.BlockSpec((ROW_TILE, D_MODEL), lambda i: (i, 0)),
        out_shape=jax.ShapeDtypeStruct((n, D_MODEL), F32),
        compiler_params=pltpu.CompilerParams(dimension_semantics=("parallel",)),
        name="entry_ln",
    )(x2, g, b)


def _proj_kernel(x_ref, wb_ref, ws_ref, ob_ref, os_ref, xb_ref):
    @pl.when(pl.program_id(1) == 0)
    def _():
        xb = x_ref[...].astype(BF16)
        xb_ref[...] = xb
        os_ref[...] = jnp.dot(xb, ws_ref[...], preferred_element_type=F32)

    ob_ref[...] = jnp.dot(xb_ref[...], wb_ref[...], preferred_element_type=F32).astype(BF16)


def _proj(x2, w_big, w_small):
    n = x2.shape[0]
    return pl.pallas_call(
        _proj_kernel,
        grid=(n // ROW_TILE, BIG_COLS // PROJ_COLS),
        in_specs=[pl.BlockSpec((ROW_TILE, D_MODEL), lambda i, j: (i, 0)),
                  pl.BlockSpec((D_MODEL, PROJ_COLS), lambda i, j: (0, j)),
                  pl.BlockSpec((D_MODEL, LANES), lambda i, j: (0, 0))],
        out_specs=[pl.BlockSpec((ROW_TILE, PROJ_COLS), lambda i, j: (i, j)),
                   pl.BlockSpec((ROW_TILE, LANES), lambda i, j: (i, 0))],
        out_shape=[jax.ShapeDtypeStruct((n, BIG_COLS), BF16),
                   jax.ShapeDtypeStruct((n, LANES), F32)],
        scratch_shapes=[pltpu.VMEM((ROW_TILE, D_MODEL), BF16)],
        compiler_params=pltpu.CompilerParams(
            dimension_semantics=("parallel", "arbitrary"), vmem_limit_bytes=VMEM_LIMIT),
        name="in_proj",
    )(x2, w_big, w_small)


def _mixer_kernel(alpha, x_ref, pb_ref, ps_ref, cw_ref, cb_ref, gbias_ref, aneg_ref,
                  dskip_ref, mng_ref, sng_ref, pa_ref, pbw_ref, wo_ref, lng_ref, lnb_ref, o_ref,
                  stage_ref, car_ref, mst_ref, mm_ref, sst_ref, ha_ref, hb_ref):
    L = TILE
    neg_inf = F32(-jnp.inf)

    @pl.when(pl.program_id(1) == 0)
    def _():
        car_ref[...] = jnp.zeros_like(car_ref)
        mst_ref[...] = jnp.zeros_like(mst_ref)
        sst_ref[...] = jnp.zeros_like(sst_ref)
        mm_ref[...] = jnp.full_like(mm_ref, NEG_BIG)

    ps = ps_ref[...] + gbias_ref[...]
    lane = lax.broadcasted_iota(jnp.int32, (L, LANES), 1)
    tail = jnp.log1p(jnp.exp(-jnp.abs(ps)))
    log_sig = jnp.minimum(ps, 0.0) - tail
    soft_plus = jnp.maximum(ps, 0.0) + tail
    is_f = (lane >= LANE_F) & (lane < LANE_DT)
    to_cum = jnp.where(is_f, log_sig, soft_plus * aneg_ref[...])
    causal = _token_index((L, 1), 0) >= _token_index((1, L), 1)
    tril = jnp.where(causal, 1.0, 0.0).astype(F32)
    cum = jnp.dot(tril, to_cum, precision=lax.Precision.HIGHEST, preferred_element_type=F32)
    packed = jnp.where(lane < LANE_F, ps, cum)
    packed_t = packed.T

    stage_ref[MIX_PREFIX:, :2 * M_QK] = pb_ref[:, OFF_Q:OFF_V].astype(F32)
    stage_ref[MIX_PREFIX:, 2 * M_QK:] = pb_ref[:, OFF_XBC:OFF_XBC + S_XBC].astype(F32)
    _fill_conv_prefix(stage_ref, car_ref, M_CONV)
    qk = _silu(_conv_taps(stage_ref, cw_ref, cb_ref, M_CONV, 0, 2 * M_QK))
    xbc = _silu(_conv_taps(stage_ref, cw_ref, cb_ref, S_CONV, 2 * M_QK, MIX_CONV_COLS))

    q_all = qk[:, :M_QK]
    k_all = qk[:, M_QK:] * (M_DQK ** -0.5)
    k_t = k_all.T
    lane_pair = lax.broadcasted_iota(jnp.int32, (L, 2 * M_DQK), 1)
    ones_col = jnp.where(lane_pair == 0, 1.0, 0.0).astype(BF16)

    for h in range(M_HEADS):
        pair, half = divmod(h, 2)
        b_col = packed[:, LANE_F + h:LANE_F + h + 1]
        w_row = packed_t[LANE_I + h:LANE_I + h + 1, :] - packed_t[LANE_F + h:LANE_F + h + 1, :]
        w_mat = jnp.where(causal, w_row, neg_inf)
        m_prev = mm_ref[h:h + 1, 0:1]
        mm = jnp.maximum(jnp.max(w_mat, axis=1, keepdims=True), m_prev)
        decay = jnp.exp(w_mat - mm)
        s_inter = jnp.exp(m_prev - mm)
        in_head = (lane_pair >= half * M_DQK) & (lane_pair < (half + 1) * M_DQK)
        q_pair = q_all[:, pair * 2 * M_DQK:(pair + 1) * 2 * M_DQK]
        k_pair = k_all[:, pair * 2 * M_DQK:(pair + 1) * 2 * M_DQK]
        q_h = jnp.where(in_head, q_pair, 0.0).astype(BF16)
        scores = lax.dot_general(q_h, k_pair.astype(BF16), (((1,), (1,)), ((), ())),
                                 preferred_element_type=F32) * decay
        v_ext = jnp.concatenate([pb_ref[:, OFF_V + h * M_DV:OFF_V + (h + 1) * M_DV], ones_col], axis=1)
        state_pair = mst_ref[pair]
        num_den = (jnp.dot(scores.astype(BF16), v_ext, preferred_element_type=F32)
                   + s_inter * jnp.dot(q_h, state_pair.astype(BF16), preferred_element_type=F32))
        num = num_den[:, :M_DV]
        den = num_den[:, M_DV:M_DV + 1]
        h_out = num / jnp.maximum(jnp.abs(den), jnp.exp(-(b_col + mm)))
        mm_last = mm[L - 1:L, :]
        kw_t = (k_t[h * M_DQK:(h + 1) * M_DQK, :] * jnp.exp(w_row - mm_last)).astype(BF16)
        state_h = mst_ref[pair, half * M_DQK:(half + 1) * M_DQK, :]
        mst_ref[pair, half * M_DQK:(half + 1) * M_DQK, :] = (
            jnp.exp(m_prev - mm_last) * state_h + jnp.dot(kw_t, v_ext, preferred_element_type=F32))
        mm_ref[h:h + 1, :] = jnp.broadcast_to(b_col[L - 1:L, :] + mm_last, (1, LANES))
        ms = jnp.mean(h_out * h_out, axis=1, keepdims=True)
        h_n = h_out * lax.rsqrt(ms + RMS_EPS) * mng_ref[:, h * M_DV:(h + 1) * M_DV]
        o_gate = jax.nn.sigmoid(pb_ref[:, OFF_O + h * M_DV:OFF_O + (h + 1) * M_DV].astype(F32))
        ha_ref[:, h * M_DV:(h + 1) * M_DV] = (o_gate * h_n).astype(BF16)

    b_t = xbc[:, S_INNER:S_INNER + S_BC].T
    gw = S_INNER // S_GROUPS
    hpg = S_HEADS // S_GROUPS
    lane_g = lax.broadcasted_iota(jnp.int32, (L, gw), 1)
    lane_g_row = lax.broadcasted_iota(jnp.int32, (1, gw), 1)

    for g in range(S_GROUPS):
        xs_g = xbc[:, g * gw:(g + 1) * gw]
        bm_g = xbc[:, S_INNER + g * S_STATE:S_INNER + (g + 1) * S_STATE].astype(BF16)
        cm_g = xbc[:, S_INNER + S_BC + g * S_STATE:S_INNER + S_BC + (g + 1) * S_STATE].astype(BF16)
        cb = lax.dot_general(cm_g, bm_g, (((1,), (1,)), ((), ())), preferred_element_type=F32)
        y_off = jnp.dot(cm_g, sst_ref[g].astype(BF16), preferred_element_type=F32)
        dt_g = jnp.zeros((L, gw), F32)
        acs_g = jnp.zeros((L, gw), F32)
        last_g = jnp.zeros((1, gw), F32)
        acs_cols, acs_rows = [], []
        for r in range(hpg):
            hh = g * hpg + r
            acs_col = packed[:, LANE_DT + hh:LANE_DT + hh + 1]
            acs_cols.append(acs_col)
            acs_rows.append(packed_t[LANE_DT + hh:LANE_DT + hh + 1, :])
            sel = (lane_g >= r * S_HEADDIM) & (lane_g < (r + 1) * S_HEADDIM)
            dt_g = jnp.where(sel, soft_plus[:, LANE_DT + hh:LANE_DT + hh + 1], dt_g)
            acs_g = jnp.where(sel, acs_col, acs_g)
            sel_row = (lane_g_row >= r * S_HEADDIM) & (lane_g_row < (r + 1) * S_HEADDIM)
            last_g = jnp.where(sel_row, acs_col[L - 1:L, :], last_g)
        xdt_g = xs_g * dt_g
        xdt_b = xdt_g.astype(BF16)
        y_diag = jnp.zeros((L, gw), F32)
        for r in range(hpg):
            seg = jnp.exp(jnp.where(causal, acs_cols[r] - acs_rows[r], neg_inf))
            mix = (cb * seg).astype(BF16)
            sel = (lane_g >= r * S_HEADDIM) & (lane_g < (r + 1) * S_HEADDIM)
            y_diag = y_diag + jnp.dot(mix, jnp.where(sel, xdt_b, jnp.zeros_like(xdt_b)),
                                      preferred_element_type=F32)
        y_g = y_diag + y_off * jnp.exp(acs_g) + dskip_ref[:, g * gw:(g + 1) * gw] * xs_g
        x_dec = (xdt_g * jnp.exp(last_g - acs_g)).astype(BF16)
        sst_ref[g] = (sst_ref[g] * jnp.exp(last_g)
                      + jnp.dot(b_t[g * S_STATE:(g + 1) * S_STATE, :].astype(BF16), x_dec,
                                preferred_element_type=F32))
        gz = y_g * _silu(pb_ref[:, OFF_Z + g * gw:OFF_Z + (g + 1) * gw].astype(F32))
        ms = jnp.mean(gz * gz, axis=1, keepdims=True)
        hb_ref[:, g * gw:(g + 1) * gw] = (gz * lax.rsqrt(ms + RMS_EPS)
                                          * sng_ref[:, g * gw:(g + 1) * gw]).astype(BF16)

    gate_a = jax.nn.sigmoid(pb_ref[:, OFF_GA:OFF_GA + D_MODEL].astype(F32))
    gate_b = jax.nn.sigmoid(pb_ref[:, OFF_GB:OFF_GB + D_MODEL].astype(F32))
    merged = (gate_a * jnp.dot(ha_ref[...], pa_ref[...], preferred_element_type=F32)
              + gate_b * jnp.dot(hb_ref[...], pbw_ref[...], preferred_element_type=F32))
    y = alpha * x_ref[...] + jnp.dot(merged.astype(BF16), wo_ref[...], preferred_element_type=F32)
    o_ref[...] = _layer_norm(y, lng_ref[...], lnb_ref[...])


def _mixer(alpha, x2, p_big, p_small, prm, batch, seq):
    L = TILE
    nt = seq // L
    row = lambda b, t: (b * nt + t, 0)
    const = lambda b, t: (0, 0)

    def full(a):
        return pl.BlockSpec(a.shape, const)

    args = (x2, p_big, p_small, prm["cw"], prm["cb"], prm["gbias"], prm["aneg"],
            prm["dskip"], prm["mng"], prm["sng"], prm["pa"], prm["pb"], prm["wo"], prm["ln1g"], prm["ln1b"])
    in_specs = [pl.BlockSpec((L, D_MODEL), row), pl.BlockSpec((L, BIG_COLS), row), pl.BlockSpec((L, LANES), row)]
    in_specs += [full(a) for a in args[3:]]
    return pl.pallas_call(
        functools.partial(_mixer_kernel, alpha),
        grid=(batch, nt),
        in_specs=in_specs,
        out_specs=pl.BlockSpec((L, D_MODEL), row),
        out_shape=jax.ShapeDtypeStruct((batch * seq, D_MODEL), F32),
        scratch_shapes=[
            pltpu.VMEM((MIX_PREFIX + L, MIX_CONV_COLS), F32),
            pltpu.VMEM((MIX_PREFIX, MIX_CONV_COLS), F32),
            pltpu.VMEM((M_HEADS // 2, 2 * M_DQK, 2 * M_DV), F32),
            pltpu.VMEM((M_HEADS, LANES), F32),
            pltpu.VMEM((S_GROUPS, S_STATE, S_INNER // S_GROUPS), F32),
            pltpu.VMEM((L, M_V), BF16),
            pltpu.VMEM((L, S_INNER), BF16),
        ],
        compiler_params=pltpu.CompilerParams(
            dimension_semantics=("parallel", "arbitrary"), vmem_limit_bytes=VMEM_LIMIT),
        name="mixer",
    )(*args)


def _ffn_kernel(alpha, x_ref, wu_ref, cw_ref, cb_ref, wd_ref, lng_ref, lnb_ref, o_ref,
                stage_ref, car_ref, acc_ref):
    bw = 2 * FFN_COLS

    @pl.when(pl.program_id(1) == 0)
    def _():
        car_ref[...] = jnp.zeros_like(car_ref)

    xb = x_ref[...].astype(BF16)

    def up_proj(j):
        stage_ref[j % 2, FFN_PREFIX:, :] = jnp.dot(xb, wu_ref[:, j * bw:(j + 1) * bw],
                                                    preferred_element_type=F32)

    up_proj(0)
    for j in range(FFN_BLOCKS):
        if j + 1 < FFN_BLOCKS:
            up_proj(j + 1)
        stage = stage_ref.at[j % 2]
        _fill_conv_prefix(stage, car_ref.at[j], FFN_CONV)
        cw = cw_ref.at[:, j * bw:(j + 1) * bw]
        cb = cb_ref.at[:, j * bw:(j + 1) * bw]
        conv_g = _conv_taps(stage, cw, cb, FFN_CONV, 0, FFN_COLS)
        conv_v = _conv_taps(stage, cw, cb, FFN_CONV, FFN_COLS, bw)
        act = (_silu(conv_g) * conv_v).astype(BF16)
        part = jnp.dot(act, wd_ref[j * FFN_COLS:(j + 1) * FFN_COLS, :], preferred_element_type=F32)
        if j == 0:
            acc_ref[...] = part
        else:
            acc_ref[...] += part
    y = alpha * x_ref[...] + acc_ref[...]
    o_ref[...] = _layer_norm(y, lng_ref[...], lnb_ref[...])


def _ffn(alpha, x2, prm, batch, seq):
    nt = seq // TILE
    row = lambda b, t: (b * nt + t, 0)
    const = lambda b, t: (0, 0)

    def full(a):
        return pl.BlockSpec(a.shape, const)

    args = (x2, prm["wu"], prm["fcw"], prm["fcb"], prm["wd"], prm["ln2g"], prm["ln2b"])
    return pl.pallas_call(
        functools.partial(_ffn_kernel, alpha),
        grid=(batch, nt),
        in_specs=[pl.BlockSpec((TILE, D_MODEL), row)] + [full(a) for a in args[1:]],
        out_specs=pl.BlockSpec((TILE, D_MODEL), row),
        out_shape=jax.ShapeDtypeStruct((batch * seq, D_MODEL), F32),
        scratch_shapes=[pltpu.VMEM((2, FFN_PREFIX + TILE, 2 * FFN_COLS), F32),
                        pltpu.VMEM((FFN_BLOCKS, FFN_PREFIX, 2 * FFN_COLS), F32),
                        pltpu.VMEM((TILE, D_MODEL), F32)],
        compiler_params=pltpu.CompilerParams(
            dimension_semantics=("parallel", "arbitrary"), vmem_limit_bytes=VMEM_LIMIT),
        name="conv_ffn",
    )(*args)


def _block_interleave(gate, value):
    pad = D_FF_PAD - D_FF
    r = gate.shape[0]
    g = jnp.pad(gate, ((0, 0), (0, pad))).reshape(r, FFN_BLOCKS, FFN_COLS)
    v = jnp.pad(value, ((0, 0), (0, pad))).reshape(r, FFN_BLOCKS, FFN_COLS)
    return jnp.concatenate([g, v], axis=2).reshape(r, FFN_BLOCKS * 2 * FFN_COLS)


def _layer_params(l, w_in, m_conv_w, m_conv_b, m_i_bias, m_f_bias, m_norm_g, s_conv_w, s_conv_b, s_dt_bias,
                  s_a_log, s_d, s_norm_g, p_a, p_b, w_out, ln1_g, ln1_b, w_up, f_conv_w, f_conv_b, w_down,
                  ln2_g, ln2_b):
    offs = [0]
    for s in IN_SIZES:
        offs.append(offs[-1] + s)
    w = w_in[l]
    part = [w[:, offs[i]:offs[i + 1]] for i in range(len(IN_SIZES))]
    wq, wk, wv, wo_, wi, wf, wz, wxbc, wdt, wga, wgb = part
    row = lambda a: a.reshape(1, -1).astype(F32)
    pad_small = LANES - 2 * M_HEADS - S_HEADS
    fcb = row(f_conv_b[l])
    prm = {
        "w_big": jnp.concatenate([wq, wk, wv, wo_, wz, wxbc, wga, wgb], axis=1).astype(BF16),
        "w_small": jnp.pad(jnp.concatenate([wi, wf, wdt], axis=1), ((0, 0), (0, pad_small))).astype(BF16),
        "cw": jnp.concatenate([m_conv_w[l], s_conv_w[l]], axis=1).astype(F32),
        "cb": jnp.concatenate([row(m_conv_b[l]), row(s_conv_b[l])], axis=1),
        "gbias": jnp.pad(jnp.concatenate([m_i_bias[l], m_f_bias[l], s_dt_bias[l]]).astype(F32),
                         (0, pad_small)).reshape(1, LANES),
        "aneg": jnp.pad(-jnp.exp(s_a_log[l].astype(F32)), (LANE_DT, pad_small)).reshape(1, LANES),
        "dskip": jnp.repeat(s_d[l].astype(F32), S_HEADDIM).reshape(1, S_INNER),
        "mng": row(m_norm_g[l]), "sng": row(s_norm_g[l]),
        "pa": p_a[l].astype(BF16), "pb": p_b[l].astype(BF16), "wo": w_out[l].astype(BF16),
        "ln1g": row(ln1_g[l]), "ln1b": row(ln1_b[l]),
        "wu": _block_interleave(w_up[l][:, :D_FF], w_up[l][:, D_FF:]).astype(BF16),
        "fcw": _block_interleave(f_conv_w[l][:, :D_FF], f_conv_w[l][:, D_FF:]).astype(F32),
        "fcb": _block_interleave(fcb[:, :D_FF], fcb[:, D_FF:]),
        "wd": jnp.pad(w_down[l], ((0, D_FF_PAD - D_FF), (0, 0))).astype(BF16),
        "ln2g": row(ln2_g[l]), "ln2b": row(ln2_b[l]),
    }
    return prm


def _to_tile_order(x, batch, seq):
    nt = seq // TILE
    x5 = x.reshape(batch, nt, SUBLANES, TILE_GROUPS, D_MODEL)
    return jnp.transpose(x5, (0, 1, 3, 2, 4)).reshape(batch * seq, D_MODEL)


def _from_tile_order(x2, batch, seq):
    nt = seq // TILE
    x5 = x2.reshape(batch, nt, TILE_GROUPS, SUBLANES, D_MODEL)
    return jnp.transpose(x5, (0, 1, 3, 2, 4)).reshape(batch, seq, D_MODEL)


def kernel(x, in_ln_g, in_ln_b, w_in, m_conv_w, m_conv_b, m_i_bias, m_f_bias, m_norm_g, s_conv_w, s_conv_b, s_dt_bias, s_a_log, s_d, s_norm_g, p_a, p_b, w_out, ln1_g, ln1_b, w_up, f_conv_w, f_conv_b, w_down, ln2_g, ln2_b):
    batch, seq, d_model = x.shape
    depth = w_in.shape[0]
    assert d_model == D_MODEL and w_in.shape[2] == sum(IN_SIZES)
    assert seq % TILE == 0 and (batch * seq) % ROW_TILE == 0
    alpha = float((2 * depth) ** 0.25)
    x2 = _entry_ln(_to_tile_order(x.astype(F32), batch, seq),
                   in_ln_g.reshape(1, -1).astype(F32), in_ln_b.reshape(1, -1).astype(F32))
    for l in range(depth):
        prm = _layer_params(l, w_in, m_conv_w, m_conv_b, m_i_bias, m_f_bias, m_norm_g, s_conv_w, s_conv_b,
                            s_dt_bias, s_a_log, s_d, s_norm_g, p_a, p_b, w_out, ln1_g, ln1_b, w_up, f_conv_w,
                            f_conv_b, w_down, ln2_g, ln2_b)
        p_big, p_small = _proj(x2, prm["w_big"], prm["w_small"])
        x2 = _mixer(alpha, x2, p_big, p_small, prm, batch, seq)
        x2 = _ffn(alpha, x2, prm, batch, seq)
    return _from_tile_order(x2, batch, seq).astype(x.dtype)
```

```python
import functools
import math

import jax
import jax.numpy as jnp
from jax import lax
from jax.experimental import pallas as pl
from jax.experimental.pallas import tpu as pltpu

F32 = jnp.float32
BF16 = jnp.bfloat16

D_MODEL = 1024
M_HEADS = 8
M_DQK = 64
M_DV = 128
M_QK = M_HEADS * M_DQK
M_V = M_HEADS * M_DV
M_CONV = 4
S_HEADDIM = 64
S_INNER = D_MODEL
S_HEADS = S_INNER // S_HEADDIM
S_GROUPS = 4
S_STATE = 128
S_CONV = 4
S_BC = S_GROUPS * S_STATE
S_XBC = S_INNER + 2 * S_BC
D_FF = 2752
FFN_CONV = 3
IN_SIZES = (M_QK, M_QK, M_V, M_V, M_HEADS, M_HEADS, S_INNER, S_XBC, S_HEADS, D_MODEL, D_MODEL)
LN_EPS = 1e-5
RMS_EPS = 1e-6
NEG_BIG = -1e30
LOG2E = math.log2(math.e)

LANES = 128
SUBLANES = 8
D_FF_PAD = 2816
TILE = 256
TILE_GROUPS = TILE // SUBLANES
FFN_COLS = 256
FFN_BLOCKS = D_FF_PAD // FFN_COLS
ROW_TILE = 1024
PROJ_COLS = 2048
VMEM_LIMIT = 56 * 1024 * 1024
MIX_CONV_COLS = 2 * M_QK + S_XBC
MIX_CONV_BLOCK = 512
MIX_PREFIX = (M_CONV - 1) * SUBLANES
FFN_PREFIX = (FFN_CONV - 1) * SUBLANES

OFF_Q = 0
OFF_K = OFF_Q + M_QK
OFF_V = OFF_K + M_QK
OFF_O = OFF_V + M_V
OFF_Z = OFF_O + M_V
OFF_XBC = OFF_Z + S_INNER
OFF_GA = OFF_XBC + S_XBC
OFF_GB = OFF_GA + D_MODEL
BIG_COLS = OFF_GB + D_MODEL
LANE_I = 0
LANE_F = LANE_I + M_HEADS
LANE_DT = LANE_F + M_HEADS


def _layer_norm(y, g, b):
    mu = jnp.mean(y, axis=-1, keepdims=True)
    yc = y - mu
    var = jnp.mean(yc * yc, axis=-1, keepdims=True)
    return yc * lax.rsqrt(var + LN_EPS) * g + b


def _sigmoid(x):
    return 0.5 * jnp.tanh(0.5 * x) + 0.5


def _silu(x):
    h = 0.5 * x
    return h + h * jnp.tanh(h)


def _token_index(n):
    p = jnp.arange(n, dtype=jnp.int32)
    return (p & (SUBLANES - 1)) * TILE_GROUPS + (p >> 3)


def _causal_tables():
    tok = _token_index(TILE)
    causal = tok[:, None] >= tok[None, :]
    return (jnp.where(causal, 0.0, -jnp.inf).astype(F32), jnp.where(causal, 1.0, 0.0).astype(F32))


def _fill_conv_prefix(stage, carry, width):
    n_pre = width - 1
    row8 = lax.broadcasted_iota(jnp.int32, (SUBLANES, stage.shape[1]), 0)
    for g in range(n_pre):
        r0 = TILE + g * SUBLANES
        cur = stage[r0:r0 + SUBLANES, :]
        prev = carry[g * SUBLANES:(g + 1) * SUBLANES, :]
        stage[g * SUBLANES:(g + 1) * SUBLANES, :] = jnp.where(
            row8 == 0, pltpu.roll(prev, 1, axis=0), pltpu.roll(cur, 1, axis=0))
        carry[g * SUBLANES:(g + 1) * SUBLANES, :] = cur


def _conv_taps(stage, w_ref, b_ref, width, c0, c1):
    p = (width - 1) * SUBLANES
    acc = b_ref[:, c0:c1] + w_ref[width - 1:width, c0:c1] * stage[p:p + TILE, c0:c1]
    for k in range(1, width):
        r0 = p - k * SUBLANES
        acc = acc + w_ref[width - 1 - k:width - k, c0:c1] * stage[r0:r0 + TILE, c0:c1]
    return acc


def _ln_kernel(x_ref, g_ref, b_ref, o_ref):
    o_ref[...] = _layer_norm(x_ref[...], g_ref[...], b_ref[...])


def _entry_ln(x2, g, b):
    n = x2.shape[0]
    return pl.pallas_call(
        _ln_kernel,
        grid=(n // ROW_TILE,),
        in_specs=[pl.BlockSpec((ROW_TILE, D_MODEL), lambda i: (i, 0)),
                  pl.BlockSpec((1, D_MODEL), lambda i: (0, 0)),
                  pl.BlockSpec((1, D_MODEL), lambda i: (0, 0))],
        out_specs=pl.BlockSpec((ROW_TILE, D_MODEL), lambda i: (i, 0)),
        out_shape=jax.ShapeDtypeStruct((n, D_MODEL), F32),
        compiler_params=pltpu.CompilerParams(dimension_semantics=("parallel",)),
        name="entry_ln",
    )(x2, g, b)


def _proj_kernel(x_ref, wb_ref, ws_ref, ob_ref, os_ref, xb_ref):
    @pl.when(pl.program_id(1) == 0)
    def _():
        xb = x_ref[...].astype(BF16)
        xb_ref[...] = xb
        os_ref[...] = jnp.dot(xb, ws_ref[...], preferred_element_type=F32)

    ob_ref[...] = jnp.dot(xb_ref[...], wb_ref[...], preferred_element_type=F32).astype(BF16)


def _proj(x2, w_big, w_small):
    n = x2.shape[0]
    return pl.pallas_call(
        _proj_kernel,
        grid=(n // ROW_TILE, BIG_COLS // PROJ_COLS),
        in_specs=[pl.BlockSpec((ROW_TILE, D_MODEL), lambda i, j: (i, 0)),
                  pl.BlockSpec((D_MODEL, PROJ_COLS), lambda i, j: (0, j)),
                  pl.BlockSpec((D_MODEL, LANES), lambda i, j: (0, 0))],
        out_specs=[pl.BlockSpec((ROW_TILE, PROJ_COLS), lambda i, j: (i, j)),
                   pl.BlockSpec((ROW_TILE, LANES), lambda i, j: (i, 0))],
        out_shape=[jax.ShapeDtypeStruct((n, BIG_COLS), BF16),
                   jax.ShapeDtypeStruct((n, LANES), F32)],
        scratch_shapes=[pltpu.VMEM((ROW_TILE, D_MODEL), BF16)],
        compiler_params=pltpu.CompilerParams(
            dimension_semantics=("parallel", "arbitrary"), vmem_limit_bytes=VMEM_LIMIT),
        name="in_proj",
    )(x2, w_big, w_small)


def _mixer_kernel(alpha, x_ref, pb_ref, ps_ref, mask_ref, tril_ref, cw_ref, cb_ref, gbias_ref, aneg_ref,
                  dskip_ref, mng_ref, sng_ref, pa_ref, pbw_ref, wo_ref, lng_ref, lnb_ref, o_ref,
                  stage_ref, car_ref, mst_ref, mm_ref, sst_ref,
                  qb_ref, kb_ref, kt_ref, xs_ref, bm_ref, cm_ref, bt_ref, ha_ref, hb_ref):
    L = TILE

    @pl.when(pl.program_id(1) == 0)
    def _():
        car_ref[...] = jnp.zeros_like(car_ref)
        mst_ref[...] = jnp.zeros_like(mst_ref)
        sst_ref[...] = jnp.zeros_like(sst_ref)
        mm_ref[...] = jnp.full_like(mm_ref, NEG_BIG * LOG2E)

    stage_ref[MIX_PREFIX:, :2 * M_QK] = pb_ref[:, OFF_Q:OFF_V].astype(F32)
    stage_ref[MIX_PREFIX:, 2 * M_QK:] = pb_ref[:, OFF_XBC:OFF_XBC + S_XBC].astype(F32)
    _fill_conv_prefix(stage_ref, car_ref, M_CONV)

    ps = ps_ref[...] + gbias_ref[...]
    lane = lax.broadcasted_iota(jnp.int32, (L, LANES), 1)
    tail = jnp.log1p(jnp.exp(-jnp.abs(ps)))
    log_sig = jnp.minimum(ps, 0.0) - tail
    soft_plus = jnp.maximum(ps, 0.0) + tail
    is_f = (lane >= LANE_F) & (lane < LANE_DT)
    to_cum = jnp.where(is_f, log_sig, soft_plus * aneg_ref[...])
    cum = jnp.dot(tril_ref[...], to_cum, precision=lax.Precision.HIGHEST, preferred_element_type=F32)
    packed = jnp.where(lane < LANE_F, ps, cum) * LOG2E
    packed_t = packed.T

    w_all = pltpu.roll(packed, LANE_F - LANE_I, axis=1) - packed
    row8 = lax.broadcasted_iota(jnp.int32, (SUBLANES, LANES), 0)
    run, cur = [], None
    for j in range(TILE_GROUPS):
        grp = w_all[j * SUBLANES:(j + 1) * SUBLANES, :]
        cur = grp if cur is None else jnp.maximum(cur, grp)
        run.append(cur)
    before = jnp.where(row8 >= 1, pltpu.roll(run[-1], 1, axis=0), -jnp.inf)
    for sh in (1, 2, 4):
        before = jnp.maximum(before, jnp.where(row8 >= sh, pltpu.roll(before, sh, axis=0), -jnp.inf))
    m_prev_all = mm_ref[...]
    before = jnp.maximum(before, m_prev_all)
    mm_all = jnp.concatenate([jnp.maximum(r, before) for r in run], axis=0)
    s_inter_all = jnp.exp2(m_prev_all[0:1, :] - mm_all)
    floor_all = jnp.exp2(-(packed + mm_all))
    mm_last_all = mm_all[L - 1:L, :]
    mm_ref[...] = jnp.broadcast_to(packed[L - 1:L, :] + mm_last_all, (SUBLANES, LANES))

    def lane_rep(a, c):
        return jnp.broadcast_to(a[:, c:c + 1], (L, LANES))

    mm_rep = [lane_rep(mm_all, LANE_F + h) for h in range(M_HEADS)]
    s_inter_rep = [lane_rep(s_inter_all, LANE_F + h) for h in range(M_HEADS)]
    floor_rep = [lane_rep(floor_all, LANE_F + h) for h in range(M_HEADS)]
    acs_rep = [lane_rep(packed, LANE_DT + hh) for hh in range(S_HEADS)]
    dt_rep = [lane_rep(soft_plus, LANE_DT + hh) for hh in range(S_HEADS)]

    def conv_block(i):
        return _silu(_conv_taps(stage_ref, cw_ref, cb_ref, M_CONV, i * MIX_CONV_BLOCK, (i + 1) * MIX_CONV_BLOCK))

    qb_ref[...] = conv_block(0).astype(BF16)
    k_all = conv_block(1) * (M_DQK ** -0.5)
    kb_ref[...] = k_all.astype(BF16)
    kt_ref[...] = k_all.T
    xs_ref[:, :MIX_CONV_BLOCK] = conv_block(2)
    xs_ref[:, MIX_CONV_BLOCK:] = conv_block(3)
    b_all = conv_block(4)
    bm_ref[...] = b_all.astype(BF16)
    bt_ref[...] = b_all.T.astype(BF16)
    cm_ref[...] = conv_block(5).astype(BF16)

    lane_pair = lax.broadcasted_iota(jnp.int32, (L, 2 * M_DQK), 1)
    ones_blk = jnp.ones((L, M_DV), BF16)
    state_pairs = [mst_ref[p] for p in range(M_HEADS // 2)]
    for h in range(M_HEADS):
        pair, half = divmod(h, 2)
        c = LANE_F + h
        w_row = packed_t[LANE_I + h:LANE_I + h + 1, :] - packed_t[c:c + 1, :]
        mm_2 = jnp.concatenate([mm_rep[h], mm_rep[h]], axis=1)
        decay = jnp.exp2((w_row + mask_ref[...]) - mm_2)
        in_head = (lane_pair >= half * M_DQK) & (lane_pair < (half + 1) * M_DQK)
        q_pair = qb_ref[:, pair * 2 * M_DQK:(pair + 1) * 2 * M_DQK]
        k_pair = kb_ref[:, pair * 2 * M_DQK:(pair + 1) * 2 * M_DQK]
        q_h = jnp.where(in_head, q_pair, jnp.zeros_like(q_pair))
        scores = lax.dot_general(q_h, k_pair, (((1,), (1,)), ((), ())), preferred_element_type=F32) * decay
        v_ext = jnp.concatenate([pb_ref[:, OFF_V + h * M_DV:OFF_V + (h + 1) * M_DV], ones_blk], axis=1)
        state_pair = state_pairs[pair]
        s_2 = jnp.concatenate([s_inter_rep[h], s_inter_rep[h]], axis=1)
        num_den = (jnp.dot(scores.astype(BF16), v_ext, preferred_element_type=F32)
                   + s_2 * jnp.dot(q_h, state_pair.astype(BF16), preferred_element_type=F32))
        kw_t = (kt_ref[h * M_DQK:(h + 1) * M_DQK, :]
                * jnp.exp2(w_row - mm_last_all[:, c:c + 1])).astype(BF16)
        mst_ref[pair, half * M_DQK:(half + 1) * M_DQK, :] = (
            s_inter_all[L - 1:L, c:c + 1] * state_pair[half * M_DQK:(half + 1) * M_DQK, :]
            + jnp.dot(kw_t, v_ext, preferred_element_type=F32))
        num = num_den[:, :M_DV]
        den = num_den[:, M_DV:]
        h_out = num / jnp.maximum(jnp.abs(den), floor_rep[h])
        ms = jnp.mean(h_out * h_out, axis=1, keepdims=True)
        h_n = h_out * lax.rsqrt(ms + RMS_EPS) * mng_ref[:, h * M_DV:(h + 1) * M_DV]
        o_gate = _sigmoid(pb_ref[:, OFF_O + h * M_DV:OFF_O + (h + 1) * M_DV].astype(F32))
        ha_ref[:, h * M_DV:(h + 1) * M_DV] = (o_gate * h_n).astype(BF16)

    ssd_states = [sst_ref[g] for g in range(S_GROUPS)]
    gw = S_INNER // S_GROUPS
    hpg = S_HEADS // S_GROUPS
    lane_g = lax.broadcasted_iota(jnp.int32, (L, gw), 1)
    low_half = lax.broadcasted_iota(jnp.int32, (L, LANES), 1) < S_HEADDIM
    assert hpg == 4 and 2 * S_HEADDIM == LANES

    for g in range(S_GROUPS):
        xs_g = xs_ref[:, g * gw:(g + 1) * gw]
        bm_g = bm_ref[:, g * S_STATE:(g + 1) * S_STATE]
        cm_g = cm_ref[:, g * S_STATE:(g + 1) * S_STATE]
        cb = lax.dot_general(cm_g, bm_g, (((1,), (1,)), ((), ())), preferred_element_type=F32)
        y_off = jnp.dot(cm_g, ssd_states[g].astype(BF16), preferred_element_type=F32)
        h0 = g * hpg
        dt_g = jnp.concatenate([jnp.where(low_half, dt_rep[h0], dt_rep[h0 + 1]),
                                jnp.where(low_half, dt_rep[h0 + 2], dt_rep[h0 + 3])], axis=1)
        acs_g = jnp.concatenate([jnp.where(low_half, acs_rep[h0], acs_rep[h0 + 1]),
                                 jnp.where(low_half, acs_rep[h0 + 2], acs_rep[h0 + 3])], axis=1)
        last_g = acs_g[L - 1:L, :]
        xdt_g = xs_g * dt_g
        xdt_b = xdt_g.astype(BF16)
        y_diag = jnp.zeros((L, gw), F32)
        for r in range(hpg):
            c = LANE_DT + h0 + r
            acs_2 = jnp.concatenate([acs_rep[h0 + r], acs_rep[h0 + r]], axis=1)
            seg = jnp.exp2((acs_2 - packed_t[c:c + 1, :]) + mask_ref[...])
            mix = (cb * seg).astype(BF16)
            sel = (lane_g >= r * S_HEADDIM) & (lane_g < (r + 1) * S_HEADDIM)
            y_diag = y_diag + jnp.dot(mix, jnp.where(sel, xdt_b, jnp.zeros_like(xdt_b)),
                                      preferred_element_type=F32)
        y_g = y_diag + y_off * jnp.exp2(acs_g) + dskip_ref[:, g * gw:(g + 1) * gw] * xs_g
        x_dec = (xdt_g * jnp.exp2(last_g - acs_g)).astype(BF16)
        sst_ref[g] = (ssd_states[g] * jnp.exp2(last_g)
                      + jnp.dot(bt_ref[g * S_STATE:(g + 1) * S_STATE, :], x_dec, preferred_element_type=F32))
        gz = y_g * _silu(pb_ref[:, OFF_Z + g * gw:OFF_Z + (g + 1) * gw].astype(F32))
        ms = jnp.mean(gz * gz, axis=1, keepdims=True)
        hb_ref[:, g * gw:(g + 1) * gw] = (gz * lax.rsqrt(ms + RMS_EPS)
                                          * sng_ref[:, g * gw:(g + 1) * gw]).astype(BF16)

    gate_a = _sigmoid(pb_ref[:, OFF_GA:OFF_GA + D_MODEL].astype(F32))
    gate_b = _sigmoid(pb_ref[:, OFF_GB:OFF_GB + D_MODEL].astype(F32))
    merged = (gate_a * jnp.dot(ha_ref[...], pa_ref[...], preferred_element_type=F32)
              + gate_b * jnp.dot(hb_ref[...], pbw_ref[...], preferred_element_type=F32))
    y = alpha * x_ref[...] + jnp.dot(merged.astype(BF16), wo_ref[...], preferred_element_type=F32)
    o_ref[...] = _layer_norm(y, lng_ref[...], lnb_ref[...])


def _mixer(alpha, x2, p_big, p_small, prm, batch, seq):
    L = TILE
    nt = seq // L
    row = lambda b, t: (b * nt + t, 0)
    const = lambda b, t: (0, 0)

    def full(a):
        return pl.BlockSpec(a.shape, const)

    mask, tril = _causal_tables()
    args = (x2, p_big, p_small, mask, tril, prm["cw"], prm["cb"], prm["gbias"], prm["aneg"],
            prm["dskip"], prm["mng"], prm["sng"], prm["pa"], prm["pb"], prm["wo"], prm["ln1g"], prm["ln1b"])
    in_specs = [pl.BlockSpec((L, D_MODEL), row), pl.BlockSpec((L, BIG_COLS), row), pl.BlockSpec((L, LANES), row)]
    in_specs += [full(a) for a in args[3:]]
    return pl.pallas_call(
        functools.partial(_mixer_kernel, alpha),
        grid=(batch, nt),
        in_specs=in_specs,
        out_specs=pl.BlockSpec((L, D_MODEL), row),
        out_shape=jax.ShapeDtypeStruct((batch * seq, D_MODEL), F32),
        scratch_shapes=[
            pltpu.VMEM((MIX_PREFIX + L, MIX_CONV_COLS), F32),
            pltpu.VMEM((MIX_PREFIX, MIX_CONV_COLS), F32),
            pltpu.VMEM((M_HEADS // 2, 2 * M_DQK, 2 * M_DV), F32),
            pltpu.VMEM((M_HEADS, LANES), F32),
            pltpu.VMEM((S_GROUPS, S_STATE, S_INNER // S_GROUPS), F32),
            pltpu.VMEM((L, M_QK), BF16),
            pltpu.VMEM((L, M_QK), BF16),
            pltpu.VMEM((M_QK, L), F32),
            pltpu.VMEM((L, S_INNER), F32),
            pltpu.VMEM((L, S_BC), BF16),
            pltpu.VMEM((L, S_BC), BF16),
            pltpu.VMEM((S_BC, L), BF16),
            pltpu.VMEM((L, M_V), BF16),
            pltpu.VMEM((L, S_INNER), BF16),
        ],
        compiler_params=pltpu.CompilerParams(
            dimension_semantics=("parallel", "arbitrary"), vmem_limit_bytes=VMEM_LIMIT),
        name="mixer",
    )(*args)


def _ffn_kernel(alpha, x_ref, wu_ref, cw_ref, cb_ref, wd_ref, lng_ref, lnb_ref, o_ref,
                stage_ref, car_ref, acc_ref):
    bw = 2 * FFN_COLS

    @pl.when(pl.program_id(1) == 0)
    def _():
        car_ref[...] = jnp.zeros_like(car_ref)

    xb = x_ref[...].astype(BF16)

    def up_proj(j):
        stage_ref[j % 2, FFN_PREFIX:, :] = jnp.dot(xb, wu_ref[:, j * bw:(j + 1) * bw],
                                                    preferred_element_type=F32)

    up_proj(0)
    for j in range(FFN_BLOCKS):
        if j + 1 < FFN_BLOCKS:
            up_proj(j + 1)
        stage = stage_ref.at[j % 2]
        _fill_conv_prefix(stage, car_ref.at[j], FFN_CONV)
        cw = cw_ref.at[:, j * bw:(j + 1) * bw]
        cb = cb_ref.at[:, j * bw:(j + 1) * bw]
        conv_g = _conv_taps(stage, cw, cb, FFN_CONV, 0, FFN_COLS)
        conv_v = _conv_taps(stage, cw, cb, FFN_CONV, FFN_COLS, bw)
        act = (_silu(conv_g) * conv_v).astype(BF16)
        part = jnp.dot(act, wd_ref[j * FFN_COLS:(j + 1) * FFN_COLS, :], preferred_element_type=F32)
        if j == 0:
            acc_ref[...] = part
        else:
            acc_ref[...] += part
    y = alpha * x_ref[...] + acc_ref[...]
    o_ref[...] = _layer_norm(y, lng_ref[...], lnb_ref[...])


def _ffn(alpha, x2, prm, batch, seq):
    nt = seq // TILE
    row = lambda b, t: (b * nt + t, 0)
    const = lambda b, t: (0, 0)

    def full(a):
        return pl.BlockSpec(a.shape, const)

    args = (x2, prm["wu"], prm["fcw"], prm["fcb"], prm["wd"], prm["ln2g"], prm["ln2b"])
    return pl.pallas_call(
        functools.partial(_ffn_kernel, alpha),
        grid=(batch, nt),
        in_specs=[pl.BlockSpec((TILE, D_MODEL), row)] + [full(a) for a in args[1:]],
        out_specs=pl.BlockSpec((TILE, D_MODEL), row),
        out_shape=jax.ShapeDtypeStruct((batch * seq, D_MODEL), F32),
        scratch_shapes=[pltpu.VMEM((2, FFN_PREFIX + TILE, 2 * FFN_COLS), F32),
                        pltpu.VMEM((FFN_BLOCKS, FFN_PREFIX, 2 * FFN_COLS), F32),
                        pltpu.VMEM((TILE, D_MODEL), F32)],
        compiler_params=pltpu.CompilerParams(
            dimension_semantics=("parallel", "arbitrary"), vmem_limit_bytes=VMEM_LIMIT),
        name="conv_ffn",
    )(*args)


def _block_interleave(gate, value):
    pad = D_FF_PAD - D_FF
    r = gate.shape[0]
    g = jnp.pad(gate, ((0, 0), (0, pad))).reshape(r, FFN_BLOCKS, FFN_COLS)
    v = jnp.pad(value, ((0, 0), (0, pad))).reshape(r, FFN_BLOCKS, FFN_COLS)
    return jnp.concatenate([g, v], axis=2).reshape(r, FFN_BLOCKS * 2 * FFN_COLS)


def _layer_params(l, w_in, m_conv_w, m_conv_b, m_i_bias, m_f_bias, m_norm_g, s_conv_w, s_conv_b, s_dt_bias,
                  s_a_log, s_d, s_norm_g, p_a, p_b, w_out, ln1_g, ln1_b, w_up, f_conv_w, f_conv_b, w_down,
                  ln2_g, ln2_b):
    offs = [0]
    for s in IN_SIZES:
        offs.append(offs[-1] + s)
    w = w_in[l]
    part = [w[:, offs[i]:offs[i + 1]] for i in range(len(IN_SIZES))]
    wq, wk, wv, wo_, wi, wf, wz, wxbc, wdt, wga, wgb = part
    row = lambda a: a.reshape(1, -1).astype(F32)
    pad_small = LANES - 2 * M_HEADS - S_HEADS
    fcb = row(f_conv_b[l])
    prm = {
        "w_big": jnp.concatenate([wq, wk, wv, wo_, wz, wxbc, wga, wgb], axis=1).astype(BF16),
        "w_small": jnp.pad(jnp.concatenate([wi, wf, wdt], axis=1), ((0, 0), (0, pad_small))).astype(BF16),
        "cw": jnp.concatenate([m_conv_w[l], s_conv_w[l]], axis=1).astype(F32),
        "cb": jnp.concatenate([row(m_conv_b[l]), row(s_conv_b[l])], axis=1),
        "gbias": jnp.pad(jnp.concatenate([m_i_bias[l], m_f_bias[l], s_dt_bias[l]]).astype(F32),
                         (0, pad_small)).reshape(1, LANES),
        "aneg": jnp.pad(-jnp.exp(s_a_log[l].astype(F32)), (LANE_DT, pad_small)).reshape(1, LANES),
        "dskip": jnp.repeat(s_d[l].astype(F32), S_HEADDIM).reshape(1, S_INNER),
        "mng": row(m_norm_g[l]), "sng": row(s_norm_g[l]),
        "pa": p_a[l].astype(BF16), "pb": p_b[l].astype(BF16), "wo": w_out[l].astype(BF16),
        "ln1g": row(ln1_g[l]), "ln1b": row(ln1_b[l]),
        "wu": _block_interleave(w_up[l][:, :D_FF], w_up[l][:, D_FF:]).astype(BF16),
        "fcw": _block_interleave(f_conv_w[l][:, :D_FF], f_conv_w[l][:, D_FF:]).astype(F32),
        "fcb": _block_interleave(fcb[:, :D_FF], fcb[:, D_FF:]),
        "wd": jnp.pad(w_down[l], ((0, D_FF_PAD - D_FF), (0, 0))).astype(BF16),
        "ln2g": row(ln2_g[l]), "ln2b": row(ln2_b[l]),
    }
    return prm


def _to_tile_order(x, batch, seq):
    nt = seq // TILE
    x5 = x.reshape(batch, nt, SUBLANES, TILE_GROUPS, D_MODEL)
    return jnp.transpose(x5, (0, 1, 3, 2, 4)).reshape(batch * seq, D_MODEL)


def _from_tile_order(x2, batch, seq):
    nt = seq // TILE
    x5 = x2.reshape(batch, nt, TILE_GROUPS, SUBLANES, D_MODEL)
    return jnp.transpose(x5, (0, 1, 3, 2, 4)).reshape(batch, seq, D_MODEL)


def kernel(x, in_ln_g, in_ln_b, w_in, m_conv_w, m_conv_b, m_i_bias, m_f_bias, m_norm_g, s_conv_w, s_conv_b, s_dt_bias, s_a_log, s_d, s_norm_g, p_a, p_b, w_out, ln1_g, ln1_b, w_up, f_conv_w, f_conv_b, w_down, ln2_g, ln2_b):
    batch, seq, d_model = x.shape
    depth = w_in.shape[0]
    assert d_model == D_MODEL and w_in.shape[2] == sum(IN_SIZES)
    assert seq % TILE == 0 and (batch * seq) % ROW_TILE == 0
    alpha = float((2 * depth) ** 0.25)
    x2 = _entry_ln(_to_tile_order(x.astype(F32), batch, seq),
                   in_ln_g.reshape(1, -1).astype(F32), in_ln_b.reshape(1, -1).astype(F32))
    for l in range(depth):
        prm = _layer_params(l, w_in, m_conv_w, m_conv_b, m_i_bias, m_f_bias, m_norm_g, s_conv_w, s_conv_b,
                            s_dt_bias, s_a_log, s_d, s_norm_g, p_a, p_b, w_out, ln1_g, ln1_b, w_up, f_conv_w,
                            f_conv_b, w_down, ln2_g, ln2_b)
        p_big, p_small = _proj(x2, prm["w_big"], prm["w_small"])
        x2 = _mixer(alpha, x2, p_big, p_small, prm, batch, seq)
        x2 = _ffn(alpha, x2, prm, batch, seq)
    return _from_tile_order(x2, batch, seq).astype(x.dtype)
```

```python
import functools
import math

import jax
import jax.numpy as jnp
from jax import lax
from jax.experimental import pallas as pl
from jax.experimental.pallas import tpu as pltpu

F32 = jnp.float32
BF16 = jnp.bfloat16

D_MODEL = 1024
M_HEADS = 8
M_DQK = 64
M_DV = 128
M_QK = M_HEADS * M_DQK
M_V = M_HEADS * M_DV
M_CONV = 4
S_HEADDIM = 64
S_INNER = D_MODEL
S_HEADS = S_INNER // S_HEADDIM
S_GROUPS = 4
S_STATE = 128
S_CONV = 4
S_BC = S_GROUPS * S_STATE
S_XBC = S_INNER + 2 * S_BC
D_FF = 2752
FFN_CONV = 3
IN_SIZES = (M_QK, M_QK, M_V, M_V, M_HEADS, M_HEADS, S_INNER, S_XBC, S_HEADS, D_MODEL, D_MODEL)
LN_EPS = 1e-5
RMS_EPS = 1e-6
NEG_BIG = -1e30
LOG2E = math.log2(math.e)

LANES = 128
SUBLANES = 8
D_FF_PAD = 2816
TILE = 256
TILE_GROUPS = TILE // SUBLANES
FFN_COLS = 256
FFN_BLOCKS = D_FF_PAD // FFN_COLS
ROW_TILE = 1024
VMEM_LIMIT = 56 * 1024 * 1024
MIX_CONV_COLS = 2 * M_QK + S_XBC
MIX_CONV_BLOCK = 512
MIX_PREFIX = (M_CONV - 1) * SUBLANES
FFN_PREFIX = (FFN_CONV - 1) * SUBLANES

OFF_Q = 0
OFF_K = OFF_Q + M_QK
OFF_V = OFF_K + M_QK
OFF_O = OFF_V + M_V
OFF_Z = OFF_O + M_V
OFF_XBC = OFF_Z + S_INNER
OFF_GA = OFF_XBC + S_XBC
OFF_GB = OFF_GA + D_MODEL
BIG_COLS = OFF_GB + D_MODEL
LANE_I = 0
LANE_F = LANE_I + M_HEADS
LANE_DT = LANE_F + M_HEADS


def _layer_norm(y, g, b):
    mu = jnp.mean(y, axis=-1, keepdims=True)
    yc = y - mu
    var = jnp.mean(yc * yc, axis=-1, keepdims=True)
    return yc * lax.rsqrt(var + LN_EPS) * g + b


def _sigmoid(x):
    return 0.5 * jnp.tanh(0.5 * x) + 0.5


def _silu(x):
    h = 0.5 * x
    return h + h * jnp.tanh(h)


def _token_index(n):
    p = jnp.arange(n, dtype=jnp.int32)
    return (p & (SUBLANES - 1)) * TILE_GROUPS + (p >> 3)


def _causal_mask():
    tok = _token_index(TILE)
    return jnp.where(tok[:, None] >= tok[None, :], 0.0, -jnp.inf).astype(F32)


def _token_scan(a, combine, identity, seed=None):
    row8 = lax.broadcasted_iota(jnp.int32, (SUBLANES, a.shape[1]), 0)
    run, cur = [], None
    for j in range(TILE_GROUPS):
        grp = a[j * SUBLANES:(j + 1) * SUBLANES, :]
        cur = grp if cur is None else combine(cur, grp)
        run.append(cur)
    before = jnp.where(row8 >= 1, pltpu.roll(run[-1], 1, axis=0), identity)
    for sh in (1, 2, 4):
        before = combine(before, jnp.where(row8 >= sh, pltpu.roll(before, sh, axis=0), identity))
    if seed is not None:
        before = combine(before, seed)
    return jnp.concatenate([combine(r, before) for r in run], axis=0)


def _fill_conv_prefix(stage, carry, width, c0, c1):
    n_pre = width - 1
    row8 = lax.broadcasted_iota(jnp.int32, (SUBLANES, c1 - c0), 0)
    for g in range(n_pre):
        r0 = TILE + g * SUBLANES
        cur = stage[r0:r0 + SUBLANES, c0:c1]
        prev = carry[g * SUBLANES:(g + 1) * SUBLANES, c0:c1]
        stage[g * SUBLANES:(g + 1) * SUBLANES, c0:c1] = jnp.where(
            row8 == 0, pltpu.roll(prev, 1, axis=0), pltpu.roll(cur, 1, axis=0))
        carry[g * SUBLANES:(g + 1) * SUBLANES, c0:c1] = cur


def _conv_taps(stage, w_ref, b_ref, width, c0, c1):
    p = (width - 1) * SUBLANES
    acc = b_ref[:, c0:c1] + w_ref[width - 1:width, c0:c1] * stage[p:p + TILE, c0:c1]
    for k in range(1, width):
        r0 = p - k * SUBLANES
        acc = acc + w_ref[width - 1 - k:width - k, c0:c1] * stage[r0:r0 + TILE, c0:c1]
    return acc


def _ln_kernel(x_ref, g_ref, b_ref, o_ref):
    o_ref[...] = _layer_norm(x_ref[...], g_ref[...], b_ref[...])


def _entry_ln(x2, g, b):
    n = x2.shape[0]
    return pl.pallas_call(
        _ln_kernel,
        grid=(n // ROW_TILE,),
        in_specs=[pl.BlockSpec((ROW_TILE, D_MODEL), lambda i: (i, 0)),
                  pl.BlockSpec((1, D_MODEL), lambda i: (0, 0)),
                  pl.BlockSpec((1, D_MODEL), lambda i: (0, 0))],
        out_specs=pl.BlockSpec((ROW_TILE, D_MODEL), lambda i: (i, 0)),
        out_shape=jax.ShapeDtypeStruct((n, D_MODEL), F32),
        compiler_params=pltpu.CompilerParams(dimension_semantics=("parallel",)),
        name="entry_ln",
    )(x2, g, b)


def _mixer_kernel(alpha, x_ref, wb_ref, ws_ref, mask_ref, cw_ref, cb_ref, gbias_ref, aneg_ref,
                  dskip_ref, mng_ref, sng_ref, pa_ref, pbw_ref, wo_ref, lng_ref, lnb_ref, o_ref,
                  stage_ref, car_ref, mst_ref, mm_ref, sst_ref, pv_ref, pe_ref,
                  qb_ref, kb_ref, kt_ref, xs_ref, bm_ref, cm_ref, bt_ref, ha_ref, hb_ref):
    L = TILE
    qk_cols = 2 * M_QK

    @pl.when(pl.program_id(1) == 0)
    def _():
        car_ref[...] = jnp.zeros_like(car_ref)
        mst_ref[...] = jnp.zeros_like(mst_ref)
        sst_ref[...] = jnp.zeros_like(sst_ref)
        mm_ref[...] = jnp.full_like(mm_ref, NEG_BIG * LOG2E)

    xb = x_ref[...].astype(BF16)

    def proj(c0, width):
        return jnp.dot(xb, wb_ref[:, c0:c0 + width], preferred_element_type=F32)

    def proj_extra(piece):
        src = (OFF_O, OFF_Z, OFF_GA, OFF_GB)[piece // 4] + (piece % 4) * 256
        pe_ref[:, piece * 256:(piece + 1) * 256] = proj(src, 256)

    ps_raw = jnp.dot(xb, ws_ref[...], preferred_element_type=F32)
    stage_ref[MIX_PREFIX:, :qk_cols] = proj(OFF_Q, qk_cols)
    _fill_conv_prefix(stage_ref, car_ref, M_CONV, 0, qk_cols)
    stage_ref[MIX_PREFIX:, qk_cols:] = proj(OFF_XBC, S_XBC)

    ps = ps_raw + gbias_ref[...]
    lane = lax.broadcasted_iota(jnp.int32, (L, LANES), 1)
    tail = jnp.log1p(jnp.exp(-jnp.abs(ps)))
    log_sig = jnp.minimum(ps, 0.0) - tail
    soft_plus = jnp.maximum(ps, 0.0) + tail
    is_f = (lane >= LANE_F) & (lane < LANE_DT)
    to_cum = jnp.where(is_f, log_sig, soft_plus * aneg_ref[...])
    cum = _token_scan(to_cum, jnp.add, 0.0)
    packed = jnp.where(lane < LANE_F, ps, cum) * LOG2E
    packed_t = packed.T

    w_all = pltpu.roll(packed, LANE_F - LANE_I, axis=1) - packed
    m_prev_all = mm_ref[...]
    mm_all = _token_scan(w_all, jnp.maximum, -jnp.inf, seed=m_prev_all)
    s_inter_all = jnp.exp2(m_prev_all[0:1, :] - mm_all)
    floor_all = jnp.exp2(-(packed + mm_all))
    mm_last_all = mm_all[L - 1:L, :]
    mm_ref[...] = jnp.broadcast_to(packed[L - 1:L, :] + mm_last_all, (SUBLANES, LANES))

    def lane_rep(a, c):
        return jnp.broadcast_to(a[:, c:c + 1], (L, LANES))

    mm_rep = [lane_rep(mm_all, LANE_F + h) for h in range(M_HEADS)]
    s_inter_rep = [lane_rep(s_inter_all, LANE_F + h) for h in range(M_HEADS)]
    floor_rep = [lane_rep(floor_all, LANE_F + h) for h in range(M_HEADS)]
    acs_rep = [lane_rep(packed, LANE_DT + hh) for hh in range(S_HEADS)]
    dt_rep = [lane_rep(soft_plus, LANE_DT + hh) for hh in range(S_HEADS)]

    def conv_block(i):
        return _silu(_conv_taps(stage_ref, cw_ref, cb_ref, M_CONV, i * MIX_CONV_BLOCK, (i + 1) * MIX_CONV_BLOCK))

    pv_ref[...] = proj(OFF_V, M_V).astype(BF16)
    qb_ref[...] = conv_block(0).astype(BF16)
    k_all = conv_block(1) * (M_DQK ** -0.5)
    kb_ref[...] = k_all.astype(BF16)
    kt_ref[...] = k_all.T
    _fill_conv_prefix(stage_ref, car_ref, S_CONV, qk_cols, MIX_CONV_COLS)
    proj_extra(0)
    xs_ref[:, :MIX_CONV_BLOCK] = conv_block(2)
    proj_extra(1)
    xs_ref[:, MIX_CONV_BLOCK:] = conv_block(3)
    proj_extra(2)
    b_all = conv_block(4)
    bm_ref[...] = b_all.astype(BF16)
    bt_ref[...] = b_all.T.astype(BF16)
    proj_extra(3)
    cm_ref[...] = conv_block(5).astype(BF16)

    lane_pair = lax.broadcasted_iota(jnp.int32, (L, 2 * M_DQK), 1)
    ones_blk = jnp.ones((L, M_DV), BF16)
    state_pairs = [mst_ref[p] for p in range(M_HEADS // 2)]
    for h in range(M_HEADS):
        pair, half = divmod(h, 2)
        c = LANE_F + h
        w_row = packed_t[LANE_I + h:LANE_I + h + 1, :] - packed_t[c:c + 1, :]
        mm_2 = jnp.concatenate([mm_rep[h], mm_rep[h]], axis=1)
        decay = jnp.exp2((w_row + mask_ref[...]) - mm_2)
        in_head = (lane_pair >= half * M_DQK) & (lane_pair < (half + 1) * M_DQK)
        q_pair = qb_ref[:, pair * 2 * M_DQK:(pair + 1) * 2 * M_DQK]
        k_pair = kb_ref[:, pair * 2 * M_DQK:(pair + 1) * 2 * M_DQK]
        q_h = jnp.where(in_head, q_pair, jnp.zeros_like(q_pair))
        scores = lax.dot_general(q_h, k_pair, (((1,), (1,)), ((), ())), preferred_element_type=F32) * decay
        proj_extra(4 + h)
        v_ext = jnp.concatenate([pv_ref[:, h * M_DV:(h + 1) * M_DV], ones_blk], axis=1)
        state_pair = state_pairs[pair]
        s_2 = jnp.concatenate([s_inter_rep[h], s_inter_rep[h]], axis=1)
        num_den = (jnp.dot(scores.astype(BF16), v_ext, preferred_element_type=F32)
                   + s_2 * jnp.dot(q_h, state_pair.astype(BF16), preferred_element_type=F32))
        kw_t = (kt_ref[h * M_DQK:(h + 1) * M_DQK, :]
                * jnp.exp2(w_row - mm_last_all[:, c:c + 1])).astype(BF16)
        mst_ref[pair, half * M_DQK:(half + 1) * M_DQK, :] = (
            s_inter_all[L - 1:L, c:c + 1] * state_pair[half * M_DQK:(half + 1) * M_DQK, :]
            + jnp.dot(kw_t, v_ext, preferred_element_type=F32))
        num = num_den[:, :M_DV]
        den = num_den[:, M_DV:]
        h_out = num / jnp.maximum(jnp.abs(den), floor_rep[h])
        ms = jnp.mean(h_out * h_out, axis=1, keepdims=True)
        h_n = h_out * lax.rsqrt(ms + RMS_EPS) * mng_ref[:, h * M_DV:(h + 1) * M_DV]
        o_gate = _sigmoid(pe_ref[:, h * M_DV:(h + 1) * M_DV])
        ha_ref[:, h * M_DV:(h + 1) * M_DV] = (o_gate * h_n).astype(BF16)

    ssd_states = [sst_ref[g] for g in range(S_GROUPS)]
    gw = S_INNER // S_GROUPS
    hpg = S_HEADS // S_GROUPS
    lane_g = lax.broadcasted_iota(jnp.int32, (L, gw), 1)
    low_half = lax.broadcasted_iota(jnp.int32, (L, LANES), 1) < S_HEADDIM
    assert hpg == 4 and 2 * S_HEADDIM == LANES

    for g in range(S_GROUPS):
        xs_g = xs_ref[:, g * gw:(g + 1) * gw]
        bm_g = bm_ref[:, g * S_STATE:(g + 1) * S_STATE]
        cm_g = cm_ref[:, g * S_STATE:(g + 1) * S_STATE]
        proj_extra(12 + g)
        cb = lax.dot_general(cm_g, bm_g, (((1,), (1,)), ((), ())), preferred_element_type=F32)
        y_off = jnp.dot(cm_g, ssd_states[g].astype(BF16), preferred_element_type=F32)
        h0 = g * hpg
        dt_g = jnp.concatenate([jnp.where(low_half, dt_rep[h0], dt_rep[h0 + 1]),
                                jnp.where(low_half, dt_rep[h0 + 2], dt_rep[h0 + 3])], axis=1)
        acs_g = jnp.concatenate([jnp.where(low_half, acs_rep[h0], acs_rep[h0 + 1]),
                                 jnp.where(low_half, acs_rep[h0 + 2], acs_rep[h0 + 3])], axis=1)
        last_g = acs_g[L - 1:L, :]
        xdt_g = xs_g * dt_g
        xdt_b = xdt_g.astype(BF16)
        y_diag = jnp.zeros((L, gw), F32)
        for r in range(hpg):
            c = LANE_DT + h0 + r
            acs_2 = jnp.concatenate([acs_rep[h0 + r], acs_rep[h0 + r]], axis=1)
            seg = jnp.exp2((acs_2 - packed_t[c:c + 1, :]) + mask_ref[...])
            mix = (cb * seg).astype(BF16)
            sel = (lane_g >= r * S_HEADDIM) & (lane_g < (r + 1) * S_HEADDIM)
            y_diag = y_diag + jnp.dot(mix, jnp.where(sel, xdt_b, jnp.zeros_like(xdt_b)),
                                      preferred_element_type=F32)
        y_g = y_diag + y_off * jnp.exp2(acs_g) + dskip_ref[:, g * gw:(g + 1) * gw] * xs_g
        x_dec = (xdt_g * jnp.exp2(last_g - acs_g)).astype(BF16)
        sst_ref[g] = (ssd_states[g] * jnp.exp2(last_g)
                      + jnp.dot(bt_ref[g * S_STATE:(g + 1) * S_STATE, :], x_dec, preferred_element_type=F32))
        gz = y_g * _silu(pe_ref[:, D_MODEL + g * gw:D_MODEL + (g + 1) * gw])
        ms = jnp.mean(gz * gz, axis=1, keepdims=True)
        hb_ref[:, g * gw:(g + 1) * gw] = (gz * lax.rsqrt(ms + RMS_EPS)
                                          * sng_ref[:, g * gw:(g + 1) * gw]).astype(BF16)

    gate_a = _sigmoid(pe_ref[:, 2 * D_MODEL:3 * D_MODEL])
    gate_b = _sigmoid(pe_ref[:, 3 * D_MODEL:])
    merged = (gate_a * jnp.dot(ha_ref[...], pa_ref[...], preferred_element_type=F32)
              + gate_b * jnp.dot(hb_ref[...], pbw_ref[...], preferred_element_type=F32))
    y = alpha * x_ref[...] + jnp.dot(merged.astype(BF16), wo_ref[...], preferred_element_type=F32)
    o_ref[...] = _layer_norm(y, lng_ref[...], lnb_ref[...])


def _mixer(alpha, x2, prm, batch, seq):
    L = TILE
    nt = seq // L
    row = lambda b, t: (b * nt + t, 0)
    const = lambda b, t: (0, 0)

    def full(a):
        return pl.BlockSpec(a.shape, const, pipeline_mode=pl.Buffered(1))

    args = (x2, prm["w_big"], prm["w_small"], _causal_mask(), prm["cw"], prm["cb"], prm["gbias"], prm["aneg"],
            prm["dskip"], prm["mng"], prm["sng"], prm["pa"], prm["pb"], prm["wo"], prm["ln1g"], prm["ln1b"])
    in_specs = [pl.BlockSpec((L, D_MODEL), row)] + [full(a) for a in args[1:]]
    return pl.pallas_call(
        functools.partial(_mixer_kernel, alpha),
        grid=(batch, nt),
        in_specs=in_specs,
        out_specs=pl.BlockSpec((L, D_MODEL), row),
        out_shape=jax.ShapeDtypeStruct((batch * seq, D_MODEL), F32),
        scratch_shapes=[
            pltpu.VMEM((MIX_PREFIX + L, MIX_CONV_COLS), F32),
            pltpu.VMEM((MIX_PREFIX, MIX_CONV_COLS), F32),
            pltpu.VMEM((M_HEADS // 2, 2 * M_DQK, 2 * M_DV), F32),
            pltpu.VMEM((M_HEADS, LANES), F32),
            pltpu.VMEM((S_GROUPS, S_STATE, S_INNER // S_GROUPS), F32),
            pltpu.VMEM((L, M_V), BF16),
            pltpu.VMEM((L, 4 * D_MODEL), F32),
            pltpu.VMEM((L, M_QK), BF16),
            pltpu.VMEM((L, M_QK), BF16),
            pltpu.VMEM((M_QK, L), F32),
            pltpu.VMEM((L, S_INNER), F32),
            pltpu.VMEM((L, S_BC), BF16),
            pltpu.VMEM((L, S_BC), BF16),
            pltpu.VMEM((S_BC, L), BF16),
            pltpu.VMEM((L, M_V), BF16),
            pltpu.VMEM((L, S_INNER), BF16),
        ],
        compiler_params=pltpu.CompilerParams(
            dimension_semantics=("parallel", "arbitrary"), vmem_limit_bytes=VMEM_LIMIT),
        name="mixer",
    )(*args)


def _ffn_kernel(alpha, x_ref, wu_ref, cw_ref, cb_ref, wd_ref, lng_ref, lnb_ref, o_ref,
                stage_ref, car_ref, acc_ref):
    bw = 2 * FFN_COLS

    @pl.when(pl.program_id(1) == 0)
    def _():
        car_ref[...] = jnp.zeros_like(car_ref)

    xb = x_ref[...].astype(BF16)

    def up_proj(j):
        stage_ref[j % 2, FFN_PREFIX:, :] = jnp.dot(xb, wu_ref[:, j * bw:(j + 1) * bw],
                                                    preferred_element_type=F32)

    up_proj(0)
    for j in range(FFN_BLOCKS):
        if j + 1 < FFN_BLOCKS:
            up_proj(j + 1)
        stage = stage_ref.at[j % 2]
        _fill_conv_prefix(stage, car_ref.at[j], FFN_CONV, 0, bw)
        cw = cw_ref.at[:, j * bw:(j + 1) * bw]
        cb = cb_ref.at[:, j * bw:(j + 1) * bw]
        conv_g = _conv_taps(stage, cw, cb, FFN_CONV, 0, FFN_COLS)
        conv_v = _conv_taps(stage, cw, cb, FFN_CONV, FFN_COLS, bw)
        act = (_silu(conv_g) * conv_v).astype(BF16)
        part = jnp.dot(act, wd_ref[j * FFN_COLS:(j + 1) * FFN_COLS, :], preferred_element_type=F32)
        if j == 0:
            acc_ref[...] = part
        else:
            acc_ref[...] += part
    y = alpha * x_ref[...] + acc_ref[...]
    o_ref[...] = _layer_norm(y, lng_ref[...], lnb_ref[...])


def _ffn(alpha, x2, prm, batch, seq):
    nt = seq // TILE
    row = lambda b, t: (b * nt + t, 0)
    const = lambda b, t: (0, 0)

    def full(a):
        return pl.BlockSpec(a.shape, const)

    args = (x2, prm["wu"], prm["fcw"], prm["fcb"], prm["wd"], prm["ln2g"], prm["ln2b"])
    return pl.pallas_call(
        functools.partial(_ffn_kernel, alpha),
        grid=(batch, nt),
        in_specs=[pl.BlockSpec((TILE, D_MODEL), row)] + [full(a) for a in args[1:]],
        out_specs=pl.BlockSpec((TILE, D_MODEL), row),
        out_shape=jax.ShapeDtypeStruct((batch * seq, D_MODEL), F32),
        scratch_shapes=[pltpu.VMEM((2, FFN_PREFIX + TILE, 2 * FFN_COLS), F32),
                        pltpu.VMEM((FFN_BLOCKS, FFN_PREFIX, 2 * FFN_COLS), F32),
                        pltpu.VMEM((TILE, D_MODEL), F32)],
        compiler_params=pltpu.CompilerParams(
            dimension_semantics=("parallel", "arbitrary"), vmem_limit_bytes=VMEM_LIMIT),
        name="conv_ffn",
    )(*args)


def _block_interleave(gate, value):
    pad = D_FF_PAD - D_FF
    r = gate.shape[0]
    g = jnp.pad(gate, ((0, 0), (0, pad))).reshape(r, FFN_BLOCKS, FFN_COLS)
    v = jnp.pad(value, ((0, 0), (0, pad))).reshape(r, FFN_BLOCKS, FFN_COLS)
    return jnp.concatenate([g, v], axis=2).reshape(r, FFN_BLOCKS * 2 * FFN_COLS)


def _layer_params(l, w_in, m_conv_w, m_conv_b, m_i_bias, m_f_bias, m_norm_g, s_conv_w, s_conv_b, s_dt_bias,
                  s_a_log, s_d, s_norm_g, p_a, p_b, w_out, ln1_g, ln1_b, w_up, f_conv_w, f_conv_b, w_down,
                  ln2_g, ln2_b):
    offs = [0]
    for s in IN_SIZES:
        offs.append(offs[-1] + s)
    w = w_in[l]
    part = [w[:, offs[i]:offs[i + 1]] for i in range(len(IN_SIZES))]
    wq, wk, wv, wo_, wi, wf, wz, wxbc, wdt, wga, wgb = part
    row = lambda a: a.reshape(1, -1).astype(F32)
    pad_small = LANES - 2 * M_HEADS - S_HEADS
    fcb = row(f_conv_b[l])
    prm = {
        "w_big": jnp.concatenate([wq, wk, wv, wo_, wz, wxbc, wga, wgb], axis=1).astype(BF16),
        "w_small": jnp.pad(jnp.concatenate([wi, wf, wdt], axis=1), ((0, 0), (0, pad_small))).astype(BF16),
        "cw": jnp.concatenate([m_conv_w[l], s_conv_w[l]], axis=1).astype(F32),
        "cb": jnp.concatenate([row(m_conv_b[l]), row(s_conv_b[l])], axis=1),
        "gbias": jnp.pad(jnp.concatenate([m_i_bias[l], m_f_bias[l], s_dt_bias[l]]).astype(F32),
                         (0, pad_small)).reshape(1, LANES),
        "aneg": jnp.pad(-jnp.exp(s_a_log[l].astype(F32)), (LANE_DT, pad_small)).reshape(1, LANES),
        "dskip": jnp.repeat(s_d[l].astype(F32), S_HEADDIM).reshape(1, S_INNER),
        "mng": row(m_norm_g[l]), "sng": row(s_norm_g[l]),
        "pa": p_a[l].astype(BF16), "pb": p_b[l].astype(BF16), "wo": w_out[l].astype(BF16),
        "ln1g": row(ln1_g[l]), "ln1b": row(ln1_b[l]),
        "wu": _block_interleave(w_up[l][:, :D_FF], w_up[l][:, D_FF:]).astype(BF16),
        "fcw": _block_interleave(f_conv_w[l][:, :D_FF], f_conv_w[l][:, D_FF:]).astype(F32),
        "fcb": _block_interleave(fcb[:, :D_FF], fcb[:, D_FF:]),
        "wd": jnp.pad(w_down[l], ((0, D_FF_PAD - D_FF), (0, 0))).astype(BF16),
        "ln2g": row(ln2_g[l]), "ln2b": row(ln2_b[l]),
    }
    return prm


def _to_tile_order(x, batch, seq):
    nt = seq // TILE
    x5 = x.reshape(batch, nt, SUBLANES, TILE_GROUPS, D_MODEL)
    return jnp.transpose(x5, (0, 1, 3, 2, 4)).reshape(batch * seq, D_MODEL)


def _from_tile_order(x2, batch, seq):
    nt = seq // TILE
    x5 = x2.reshape(batch, nt, TILE_GROUPS, SUBLANES, D_MODEL)
    return jnp.transpose(x5, (0, 1, 3, 2, 4)).reshape(batch, seq, D_MODEL)


def kernel(x, in_ln_g, in_ln_b, w_in, m_conv_w, m_conv_b, m_i_bias, m_f_bias, m_norm_g, s_conv_w, s_conv_b, s_dt_bias, s_a_log, s_d, s_norm_g, p_a, p_b, w_out, ln1_g, ln1_b, w_up, f_conv_w, f_conv_b, w_down, ln2_g, ln2_b):
    batch, seq, d_model = x.shape
    depth = w_in.shape[0]
    assert d_model == D_MODEL and w_in.shape[2] == sum(IN_SIZES)
    assert seq % TILE == 0 and (batch * seq) % ROW_TILE == 0
    alpha = float((2 * depth) ** 0.25)
    x2 = _entry_ln(_to_tile_order(x.astype(F32), batch, seq),
                   in_ln_g.reshape(1, -1).astype(F32), in_ln_b.reshape(1, -1).astype(F32))
    for l in range(depth):
        prm = _layer_params(l, w_in, m_conv_w, m_conv_b, m_i_bias, m_f_bias, m_norm_g, s_conv_w, s_conv_b,
                            s_dt_bias, s_a_log, s_d, s_norm_g, p_a, p_b, w_out, ln1_g, ln1_b, w_up, f_conv_w,
                            f_conv_b, w_down, ln2_g, ln2_b)
        x2 = _mixer(alpha, x2, prm, batch, seq)
        x2 = _ffn(alpha, x2, prm, batch, seq)
    return _from_tile_order(x2, batch, seq).astype(x.dtype)
```

```python
import functools
import math

import jax
import jax.numpy as jnp
from jax import lax
from jax.experimental import pallas as pl
from jax.experimental.pallas import tpu as pltpu

F32 = jnp.float32
BF16 = jnp.bfloat16

D_MODEL = 1024
M_HEADS = 8
M_DQK = 64
M_DV = 128
M_QK = M_HEADS * M_DQK
M_V = M_HEADS * M_DV
M_CONV = 4
S_HEADDIM = 64
S_INNER = D_MODEL
S_HEADS = S_INNER // S_HEADDIM
S_GROUPS = 4
S_STATE = 128
S_CONV = 4
S_BC = S_GROUPS * S_STATE
S_XBC = S_INNER + 2 * S_BC
D_FF = 2752
FFN_CONV = 3
IN_SIZES = (M_QK, M_QK, M_V, M_V, M_HEADS, M_HEADS, S_INNER, S_XBC, S_HEADS, D_MODEL, D_MODEL)
LN_EPS = 1e-5
RMS_EPS = 1e-6
NEG_BIG = -1e30
LOG2E = math.log2(math.e)

LANES = 128
SUBLANES = 8
D_FF_PAD = 2816
TILE = 256
TILE_GROUPS = TILE // SUBLANES
FFN_COLS = 256
FFN_BLOCKS = D_FF_PAD // FFN_COLS
FFN_SUBTILES = 4
MIX_SUBTILES = 1
VMEM_LIMIT = 56 * 1024 * 1024
MIX_CONV_COLS = 2 * M_QK + S_XBC
MIX_CONV_BLOCK = 512
MIX_PREFIX = (M_CONV - 1) * SUBLANES
FFN_PREFIX = (FFN_CONV - 1) * SUBLANES

OFF_Q = 0
OFF_K = OFF_Q + M_QK
OFF_V = OFF_K + M_QK
OFF_O = OFF_V + M_V
OFF_Z = OFF_O + M_V
OFF_XBC = OFF_Z + S_INNER
OFF_GA = OFF_XBC + S_XBC
OFF_GB = OFF_GA + D_MODEL
BIG_COLS = OFF_GB + D_MODEL
LANE_I = 0
LANE_F = LANE_I + M_HEADS
LANE_DT = LANE_F + M_HEADS


def _layer_norm(y, g, b):
    mu = jnp.mean(y, axis=-1, keepdims=True)
    yc = y - mu
    var = jnp.mean(yc * yc, axis=-1, keepdims=True)
    return yc * lax.rsqrt(var + LN_EPS) * g + b


def _sigmoid(x):
    return 0.5 * jnp.tanh(0.5 * x) + 0.5


def _silu(x):
    h = 0.5 * x
    return h + h * jnp.tanh(h)


def _token_index(n):
    p = jnp.arange(n, dtype=jnp.int32)
    return (p & (SUBLANES - 1)) * TILE_GROUPS + (p >> 3)


def _causal_mask():
    tok = _token_index(TILE)
    return jnp.where(tok[:, None] >= tok[None, :], 0.0, -jnp.inf).astype(F32)


def _token_scan(a, combine, identity, seed=None):
    row8 = lax.broadcasted_iota(jnp.int32, (SUBLANES, a.shape[1]), 0)
    run, cur = [], None
    for j in range(TILE_GROUPS):
        grp = a[j * SUBLANES:(j + 1) * SUBLANES, :]
        cur = grp if cur is None else combine(cur, grp)
        run.append(cur)
    before = jnp.where(row8 >= 1, pltpu.roll(run[-1], 1, axis=0), identity)
    for sh in (1, 2, 4):
        before = combine(before, jnp.where(row8 >= sh, pltpu.roll(before, sh, axis=0), identity))
    if seed is not None:
        before = combine(before, seed)
    return jnp.concatenate([combine(r, before) for r in run], axis=0)


def _fill_conv_prefix(stage, carry, width, c0, c1):
    n_pre = width - 1
    row8 = lax.broadcasted_iota(jnp.int32, (SUBLANES, c1 - c0), 0)
    for g in range(n_pre):
        r0 = TILE + g * SUBLANES
        cur = stage[r0:r0 + SUBLANES, c0:c1]
        prev = carry[g * SUBLANES:(g + 1) * SUBLANES, c0:c1]
        stage[g * SUBLANES:(g + 1) * SUBLANES, c0:c1] = jnp.where(
            row8 == 0, pltpu.roll(prev, 1, axis=0), pltpu.roll(cur, 1, axis=0))
        carry[g * SUBLANES:(g + 1) * SUBLANES, c0:c1] = cur


def _conv_taps(stage, w_ref, b_ref, width, c0, c1):
    p = (width - 1) * SUBLANES
    acc = b_ref[:, c0:c1] + w_ref[width - 1:width, c0:c1] * stage[p:p + TILE, c0:c1]
    for k in range(1, width):
        r0 = p - k * SUBLANES
        acc = acc + w_ref[width - 1 - k:width - k, c0:c1] * stage[r0:r0 + TILE, c0:c1]
    return acc


def _mixer_kernel(alpha, entry, n_in, *refs):
    x_ref, params = refs[0], refs[1:n_in]
    o_ref, scratch = refs[n_in], refs[n_in + 1:]
    _, car_ref, mst_ref, mm_ref, sst_ref = scratch[:5]

    @pl.when(pl.program_id(1) == 0)
    def _():
        car_ref[...] = jnp.zeros_like(car_ref)
        mst_ref[...] = jnp.zeros_like(mst_ref)
        sst_ref[...] = jnp.zeros_like(sst_ref)
        mm_ref[...] = jnp.full_like(mm_ref, NEG_BIG * LOG2E)

    for sub in range(MIX_SUBTILES):
        rows = pl.ds(sub * TILE, TILE)
        _mixer_tile(alpha, entry, x_ref.at[rows], *params, o_ref.at[rows], *scratch)


def _mixer_tile(alpha, entry, x_ref, ing_ref, inb_ref, wb_ref, ws_ref, mask_ref, cw_ref, cb_ref, gbias_ref, aneg_ref,
                dskip_ref, mng_ref, sng_ref, pa_ref, pbw_ref, wo_ref, lng_ref, lnb_ref, o_ref,
                stage_ref, car_ref, mst_ref, mm_ref, sst_ref, pv_ref, pe_ref,
                qb_ref, kb_ref, kt_ref, xs_ref, bm_ref, cm_ref, bt_ref, ha_ref, hb_ref):
    L = TILE
    qk_cols = 2 * M_QK

    def layer_input():
        x = x_ref[...]
        return _layer_norm(x, ing_ref[...], inb_ref[...]) if entry else x

    xb = layer_input().astype(BF16)

    def proj(c0, width):
        return jnp.dot(xb, wb_ref[:, c0:c0 + width], preferred_element_type=F32)

    def proj_extra(piece):
        src = (OFF_O, OFF_Z, OFF_GA, OFF_GB)[piece // 4] + (piece % 4) * 256
        pe_ref[:, piece * 256:(piece + 1) * 256] = proj(src, 256)

    ps_raw = jnp.dot(xb, ws_ref[...], preferred_element_type=F32)
    stage_ref[MIX_PREFIX:, :qk_cols] = proj(OFF_Q, qk_cols)
    _fill_conv_prefix(stage_ref, car_ref, M_CONV, 0, qk_cols)
    stage_ref[MIX_PREFIX:, qk_cols:] = proj(OFF_XBC, S_XBC)

    ps = ps_raw + gbias_ref[...]
    lane = lax.broadcasted_iota(jnp.int32, (L, LANES), 1)
    tail = jnp.log1p(jnp.exp(-jnp.abs(ps)))
    log_sig = jnp.minimum(ps, 0.0) - tail
    soft_plus = jnp.maximum(ps, 0.0) + tail
    is_f = (lane >= LANE_F) & (lane < LANE_DT)
    to_cum = jnp.where(is_f, log_sig, soft_plus * aneg_ref[...])
    cum = _token_scan(to_cum, jnp.add, 0.0)
    packed = jnp.where(lane < LANE_F, ps, cum) * LOG2E
    packed_t = packed.T

    w_all = pltpu.roll(packed, LANE_F - LANE_I, axis=1) - packed
    m_prev_all = mm_ref[...]
    mm_all = _token_scan(w_all, jnp.maximum, -jnp.inf, seed=m_prev_all)
    s_inter_all = jnp.exp2(m_prev_all[0:1, :] - mm_all)
    floor_all = jnp.exp2(-(packed + mm_all))
    mm_last_all = mm_all[L - 1:L, :]
    mm_ref[...] = jnp.broadcast_to(packed[L - 1:L, :] + mm_last_all, (SUBLANES, LANES))

    def lane_rep(a, c):
        return jnp.broadcast_to(a[:, c:c + 1], (L, LANES))

    mm_rep = [lane_rep(mm_all, LANE_F + h) for h in range(M_HEADS)]
    s_inter_rep = [lane_rep(s_inter_all, LANE_F + h) for h in range(M_HEADS)]
    floor_rep = [lane_rep(floor_all, LANE_F + h) for h in range(M_HEADS)]
    acs_rep = [lane_rep(packed, LANE_DT + hh) for hh in range(S_HEADS)]
    dt_rep = [lane_rep(soft_plus, LANE_DT + hh) for hh in range(S_HEADS)]

    def conv_block(i):
        return _silu(_conv_taps(stage_ref, cw_ref, cb_ref, M_CONV, i * MIX_CONV_BLOCK, (i + 1) * MIX_CONV_BLOCK))

    pv_ref[...] = proj(OFF_V, M_V).astype(BF16)
    qb_ref[...] = conv_block(0).astype(BF16)
    k_all = conv_block(1) * (M_DQK ** -0.5)
    kb_ref[...] = k_all.astype(BF16)
    kt_ref[...] = k_all.T
    _fill_conv_prefix(stage_ref, car_ref, S_CONV, qk_cols, MIX_CONV_COLS)
    proj_extra(0)
    xs_ref[:, :MIX_CONV_BLOCK] = conv_block(2)
    proj_extra(1)
    xs_ref[:, MIX_CONV_BLOCK:] = conv_block(3)
    proj_extra(2)
    b_all = conv_block(4)
    bm_ref[...] = b_all.astype(BF16)
    bt_ref[...] = b_all.T.astype(BF16)
    proj_extra(3)
    cm_ref[...] = conv_block(5).astype(BF16)

    lane_pair = lax.broadcasted_iota(jnp.int32, (L, 2 * M_DQK), 1)
    ones_blk = jnp.ones((L, M_DV), BF16)
    state_pairs = [mst_ref[p] for p in range(M_HEADS // 2)]
    for h in range(M_HEADS):
        pair, half = divmod(h, 2)
        c = LANE_F + h
        w_row = packed_t[LANE_I + h:LANE_I + h + 1, :] - packed_t[c:c + 1, :]
        mm_2 = jnp.concatenate([mm_rep[h], mm_rep[h]], axis=1)
        decay = jnp.exp2((w_row + mask_ref[...]) - mm_2)
        in_head = (lane_pair >= half * M_DQK) & (lane_pair < (half + 1) * M_DQK)
        q_pair = qb_ref[:, pair * 2 * M_DQK:(pair + 1) * 2 * M_DQK]
        k_pair = kb_ref[:, pair * 2 * M_DQK:(pair + 1) * 2 * M_DQK]
        q_h = jnp.where(in_head, q_pair, jnp.zeros_like(q_pair))
        scores = lax.dot_general(q_h, k_pair, (((1,), (1,)), ((), ())), preferred_element_type=F32) * decay
        proj_extra(4 + h)
        v_ext = jnp.concatenate([pv_ref[:, h * M_DV:(h + 1) * M_DV], ones_blk], axis=1)
        state_pair = state_pairs[pair]
        s_2 = jnp.concatenate([s_inter_rep[h], s_inter_rep[h]], axis=1)
        num_den = (jnp.dot(scores.astype(BF16), v_ext, preferred_element_type=F32)
                   + s_2 * jnp.dot(q_h, state_pair.astype(BF16), preferred_element_type=F32))
        kw_t = (kt_ref[h * M_DQK:(h + 1) * M_DQK, :]
                * jnp.exp2(w_row - mm_last_all[:, c:c + 1])).astype(BF16)
        mst_ref[pair, half * M_DQK:(half + 1) * M_DQK, :] = (
            s_inter_all[L - 1:L, c:c + 1] * state_pair[half * M_DQK:(half + 1) * M_DQK, :]
            + jnp.dot(kw_t, v_ext, preferred_element_type=F32))
        num = num_den[:, :M_DV]
        den = num_den[:, M_DV:]
        h_out = num / jnp.maximum(jnp.abs(den), floor_rep[h])
        ms = jnp.mean(h_out * h_out, axis=1, keepdims=True)
        h_n = h_out * lax.rsqrt(ms + RMS_EPS) * mng_ref[:, h * M_DV:(h + 1) * M_DV]
        o_gate = _sigmoid(pe_ref[:, h * M_DV:(h + 1) * M_DV])
        ha_ref[:, h * M_DV:(h + 1) * M_DV] = (o_gate * h_n).astype(BF16)

    ssd_states = [sst_ref[g] for g in range(S_GROUPS)]
    gw = S_INNER // S_GROUPS
    hpg = S_HEADS // S_GROUPS
    lane_g = lax.broadcasted_iota(jnp.int32, (L, gw), 1)
    low_half = lax.broadcasted_iota(jnp.int32, (L, LANES), 1) < S_HEADDIM
    assert hpg == 4 and 2 * S_HEADDIM == LANES

    for g in range(S_GROUPS):
        xs_g = xs_ref[:, g * gw:(g + 1) * gw]
        bm_g = bm_ref[:, g * S_STATE:(g + 1) * S_STATE]
        cm_g = cm_ref[:, g * S_STATE:(g + 1) * S_STATE]
        proj_extra(12 + g)
        cb = lax.dot_general(cm_g, bm_g, (((1,), (1,)), ((), ())), preferred_element_type=F32)
        y_off = jnp.dot(cm_g, ssd_states[g].astype(BF16), preferred_element_type=F32)
        h0 = g * hpg
        dt_g = jnp.concatenate([jnp.where(low_half, dt_rep[h0], dt_rep[h0 + 1]),
                                jnp.where(low_half, dt_rep[h0 + 2], dt_rep[h0 + 3])], axis=1)
        acs_g = jnp.concatenate([jnp.where(low_half, acs_rep[h0], acs_rep[h0 + 1]),
                                 jnp.where(low_half, acs_rep[h0 + 2], acs_rep[h0 + 3])], axis=1)
        last_g = acs_g[L - 1:L, :]
        xdt_g = xs_g * dt_g
        xdt_b = xdt_g.astype(BF16)
        y_diag = jnp.zeros((L, gw), F32)
        for r in range(hpg):
            c = LANE_DT + h0 + r
            acs_2 = jnp.concatenate([acs_rep[h0 + r], acs_rep[h0 + r]], axis=1)
            seg = jnp.exp2((acs_2 - packed_t[c:c + 1, :]) + mask_ref[...])
            mix = (cb * seg).astype(BF16)
            sel = (lane_g >= r * S_HEADDIM) & (lane_g < (r + 1) * S_HEADDIM)
            y_diag = y_diag + jnp.dot(mix, jnp.where(sel, xdt_b, jnp.zeros_like(xdt_b)),
                                      preferred_element_type=F32)
        y_g = y_diag + y_off * jnp.exp2(acs_g) + dskip_ref[:, g * gw:(g + 1) * gw] * xs_g
        x_dec = (xdt_g * jnp.exp2(last_g - acs_g)).astype(BF16)
        sst_ref[g] = (ssd_states[g] * jnp.exp2(last_g)
                      + jnp.dot(bt_ref[g * S_STATE:(g + 1) * S_STATE, :], x_dec, preferred_element_type=F32))
        gz = y_g * _silu(pe_ref[:, D_MODEL + g * gw:D_MODEL + (g + 1) * gw])
        ms = jnp.mean(gz * gz, axis=1, keepdims=True)
        hb_ref[:, g * gw:(g + 1) * gw] = (gz * lax.rsqrt(ms + RMS_EPS)
                                          * sng_ref[:, g * gw:(g + 1) * gw]).astype(BF16)

    gate_a = _sigmoid(pe_ref[:, 2 * D_MODEL:3 * D_MODEL])
    gate_b = _sigmoid(pe_ref[:, 3 * D_MODEL:])
    merged = (gate_a * jnp.dot(ha_ref[...], pa_ref[...], preferred_element_type=F32)
              + gate_b * jnp.dot(hb_ref[...], pbw_ref[...], preferred_element_type=F32))
    y = alpha * layer_input() + jnp.dot(merged.astype(BF16), wo_ref[...], preferred_element_type=F32)
    o_ref[...] = _layer_norm(y, lng_ref[...], lnb_ref[...])


def _mixer(alpha, entry, x2, in_ln_g, in_ln_b, prm, batch, seq):
    L = TILE
    rows = MIX_SUBTILES * L
    nt = seq // rows
    row = lambda b, t: (b * nt + t, 0)
    const = lambda b, t: (0, 0)

    def full(a):
        return pl.BlockSpec(a.shape, const, pipeline_mode=pl.Buffered(1))

    args = (x2, in_ln_g, in_ln_b, prm["w_big"], prm["w_small"], _causal_mask(), prm["cw"], prm["cb"],
            prm["gbias"], prm["aneg"], prm["dskip"], prm["mng"], prm["sng"], prm["pa"], prm["pb"], prm["wo"],
            prm["ln1g"], prm["ln1b"])
    in_specs = [pl.BlockSpec((rows, D_MODEL), row)] + [full(a) for a in args[1:]]
    return pl.pallas_call(
        functools.partial(_mixer_kernel, alpha, entry, len(args)),
        grid=(batch, nt),
        in_specs=in_specs,
        out_specs=pl.BlockSpec((rows, D_MODEL), row),
        out_shape=jax.ShapeDtypeStruct((batch * seq, D_MODEL), F32),
        scratch_shapes=[
            pltpu.VMEM((MIX_PREFIX + L, MIX_CONV_COLS), F32),
            pltpu.VMEM((MIX_PREFIX, MIX_CONV_COLS), F32),
            pltpu.VMEM((M_HEADS // 2, 2 * M_DQK, 2 * M_DV), F32),
            pltpu.VMEM((M_HEADS, LANES), F32),
            pltpu.VMEM((S_GROUPS, S_STATE, S_INNER // S_GROUPS), F32),
            pltpu.VMEM((L, M_V), BF16),
            pltpu.VMEM((L, 4 * D_MODEL), F32),
            pltpu.VMEM((L, M_QK), BF16),
            pltpu.VMEM((L, M_QK), BF16),
            pltpu.VMEM((M_QK, L), F32),
            pltpu.VMEM((L, S_INNER), F32),
            pltpu.VMEM((L, S_BC), BF16),
            pltpu.VMEM((L, S_BC), BF16),
            pltpu.VMEM((S_BC, L), BF16),
            pltpu.VMEM((L, M_V), BF16),
            pltpu.VMEM((L, S_INNER), BF16),
        ],
        compiler_params=pltpu.CompilerParams(
            dimension_semantics=("parallel", "arbitrary"), vmem_limit_bytes=VMEM_LIMIT),
        name="mixer",
    )(*args)


def _ffn_kernel(alpha, x_ref, wg_ref, wv_ref, cwg_ref, cwv_ref, cbg_ref, cbv_ref, wd_ref, lng_ref, lnb_ref,
                o_ref, stage_ref, car_ref, acc_ref):
    bw = 2 * FFN_COLS

    @pl.when(pl.program_id(1) == 0)
    def _():
        car_ref[...] = jnp.zeros_like(car_ref)

    def tile_rows(sub):
        return slice(sub * TILE, (sub + 1) * TILE)

    def up_proj(xb, j):
        cols = slice(j * FFN_COLS, (j + 1) * FFN_COLS)
        stage_ref[j % 2, FFN_PREFIX:, :FFN_COLS] = jnp.dot(xb, wg_ref[:, cols], preferred_element_type=F32)
        stage_ref[j % 2, FFN_PREFIX:, FFN_COLS:] = jnp.dot(xb, wv_ref[:, cols], preferred_element_type=F32)

    xb = x_ref[tile_rows(0), :].astype(BF16)
    up_proj(xb, 0)
    for sub in range(FFN_SUBTILES):
        for j in range(FFN_BLOCKS):
            if j + 1 < FFN_BLOCKS:
                up_proj(xb, j + 1)
            stage = stage_ref.at[j % 2]
            cols = slice(j * FFN_COLS, (j + 1) * FFN_COLS)
            _fill_conv_prefix(stage, car_ref.at[j], FFN_CONV, 0, bw)
            conv_g = _conv_taps(stage.at[:, :FFN_COLS], cwg_ref.at[:, cols], cbg_ref.at[:, cols],
                                FFN_CONV, 0, FFN_COLS)
            conv_v = _conv_taps(stage.at[:, FFN_COLS:], cwv_ref.at[:, cols], cbv_ref.at[:, cols],
                                FFN_CONV, 0, FFN_COLS)
            act = (_silu(conv_g) * conv_v).astype(BF16)
            part = jnp.dot(act, wd_ref[cols, :], preferred_element_type=F32)
            if j == 0:
                acc_ref[sub] = part
            else:
                acc_ref[sub] += part
        if sub + 1 < FFN_SUBTILES:
            xb = x_ref[tile_rows(sub + 1), :].astype(BF16)
            up_proj(xb, 0)
        y = alpha * x_ref[tile_rows(sub), :] + acc_ref[sub]
        o_ref[tile_rows(sub), :] = _layer_norm(y, lng_ref[...], lnb_ref[...])


def _ffn(alpha, x2, prm, batch, seq):
    rows = FFN_SUBTILES * TILE
    nt = seq // rows
    row = lambda b, t: (b * nt + t, 0)
    const = lambda b, t: (0, 0)

    def full(a):
        return pl.BlockSpec(a.shape, const)

    args = (x2, prm["wg"], prm["wv"], prm["cwg"], prm["cwv"], prm["cbg"], prm["cbv"], prm["wd"],
            prm["ln2g"], prm["ln2b"])
    return pl.pallas_call(
        functools.partial(_ffn_kernel, alpha),
        grid=(batch, nt),
        in_specs=[pl.BlockSpec((rows, D_MODEL), row)] + [full(a) for a in args[1:]],
        out_specs=pl.BlockSpec((rows, D_MODEL), row),
        out_shape=jax.ShapeDtypeStruct((batch * seq, D_MODEL), F32),
        scratch_shapes=[pltpu.VMEM((2, FFN_PREFIX + TILE, 2 * FFN_COLS), F32),
                        pltpu.VMEM((FFN_BLOCKS, FFN_PREFIX, 2 * FFN_COLS), F32),
                        pltpu.VMEM((FFN_SUBTILES, TILE, D_MODEL), F32)],
        compiler_params=pltpu.CompilerParams(
            dimension_semantics=("parallel", "arbitrary"), vmem_limit_bytes=VMEM_LIMIT),
        name="conv_ffn",
    )(*args)


def _pad_ff(a, dtype):
    return jnp.pad(a.astype(dtype), ((0, 0), (0, D_FF_PAD - D_FF)))


def _layer_params(l, w_in, m_conv_w, m_conv_b, m_i_bias, m_f_bias, m_norm_g, s_conv_w, s_conv_b, s_dt_bias,
                  s_a_log, s_d, s_norm_g, p_a, p_b, w_out, ln1_g, ln1_b, w_up, f_conv_w, f_conv_b, w_down,
                  ln2_g, ln2_b):
    offs = [0]
    for s in IN_SIZES:
        offs.append(offs[-1] + s)
    w = w_in[l]
    part = [w[:, offs[i]:offs[i + 1]] for i in range(len(IN_SIZES))]
    wq, wk, wv, wo_, wi, wf, wz, wxbc, wdt, wga, wgb = part
    row = lambda a: a.reshape(1, -1).astype(F32)
    pad_small = LANES - 2 * M_HEADS - S_HEADS
    fcb = row(f_conv_b[l])
    prm = {
        "w_big": jnp.concatenate([wq, wk, wv, wo_, wz, wxbc, wga, wgb], axis=1).astype(BF16),
        "w_small": jnp.pad(jnp.concatenate([wi, wf, wdt], axis=1), ((0, 0), (0, pad_small))).astype(BF16),
        "cw": jnp.concatenate([m_conv_w[l], s_conv_w[l]], axis=1).astype(F32),
        "cb": jnp.concatenate([row(m_conv_b[l]), row(s_conv_b[l])], axis=1),
        "gbias": jnp.pad(jnp.concatenate([m_i_bias[l], m_f_bias[l], s_dt_bias[l]]).astype(F32),
                         (0, pad_small)).reshape(1, LANES),
        "aneg": jnp.pad(-jnp.exp(s_a_log[l].astype(F32)), (LANE_DT, pad_small)).reshape(1, LANES),
        "dskip": jnp.repeat(s_d[l].astype(F32), S_HEADDIM).reshape(1, S_INNER),
        "mng": row(m_norm_g[l]), "sng": row(s_norm_g[l]),
        "pa": p_a[l].astype(BF16), "pb": p_b[l].astype(BF16), "wo": w_out[l].astype(BF16),
        "ln1g": row(ln1_g[l]), "ln1b": row(ln1_b[l]),
        "wg": _pad_ff(w_up[l][:, :D_FF], BF16), "wv": _pad_ff(w_up[l][:, D_FF:], BF16),
        "cwg": _pad_ff(f_conv_w[l][:, :D_FF], F32), "cwv": _pad_ff(f_conv_w[l][:, D_FF:], F32),
        "cbg": _pad_ff(fcb[:, :D_FF], F32), "cbv": _pad_ff(fcb[:, D_FF:], F32),
        "wd": jnp.pad(w_down[l].astype(BF16), ((0, D_FF_PAD - D_FF), (0, 0))),
        "ln2g": row(ln2_g[l]), "ln2b": row(ln2_b[l]),
    }
    return prm


def _to_tile_order(x, batch, seq):
    nt = seq // TILE
    x5 = x.reshape(batch, nt, SUBLANES, TILE_GROUPS, D_MODEL)
    return jnp.transpose(x5, (0, 1, 3, 2, 4)).reshape(batch * seq, D_MODEL)


def _from_tile_order(x2, batch, seq):
    nt = seq // TILE
    x5 = x2.reshape(batch, nt, TILE_GROUPS, SUBLANES, D_MODEL)
    return jnp.transpose(x5, (0, 1, 3, 2, 4)).reshape(batch, seq, D_MODEL)


def kernel(x, in_ln_g, in_ln_b, w_in, m_conv_w, m_conv_b, m_i_bias, m_f_bias, m_norm_g, s_conv_w, s_conv_b, s_dt_bias, s_a_log, s_d, s_norm_g, p_a, p_b, w_out, ln1_g, ln1_b, w_up, f_conv_w, f_conv_b, w_down, ln2_g, ln2_b):
    batch, seq, d_model = x.shape
    depth = w_in.shape[0]
    assert d_model == D_MODEL and w_in.shape[2] == sum(IN_SIZES)
    assert seq % (FFN_SUBTILES * TILE) == 0 and seq % (MIX_SUBTILES * TILE) == 0 and depth >= 1
    alpha = float((2 * depth) ** 0.25)
    x2 = _to_tile_order(x.astype(F32), batch, seq)
    in_g, in_b = in_ln_g.reshape(1, -1).astype(F32), in_ln_b.reshape(1, -1).astype(F32)
    for l in range(depth):
        prm = _layer_params(l, w_in, m_conv_w, m_conv_b, m_i_bias, m_f_bias, m_norm_g, s_conv_w, s_conv_b,
                            s_dt_bias, s_a_log, s_d, s_norm_g, p_a, p_b, w_out, ln1_g, ln1_b, w_up, f_conv_w,
                            f_conv_b, w_down, ln2_g, ln2_b)
        x2 = _mixer(alpha, l == 0, x2, in_g, in_b, prm, batch, seq)
        x2 = _ffn(alpha, x2, prm, batch, seq)
    return _from_tile_order(x2, batch, seq).astype(x.dtype)
```

```python
import functools
import math

import jax
import jax.numpy as jnp
from jax import lax
from jax.experimental import pallas as pl
from jax.experimental.pallas import tpu as pltpu

F32 = jnp.float32
BF16 = jnp.bfloat16

D_MODEL = 1024
M_HEADS = 8
M_DQK = 64
M_DV = 128
M_QK = M_HEADS * M_DQK
M_V = M_HEADS * M_DV
M_CONV = 4
S_HEADDIM = 64
S_INNER = D_MODEL
S_HEADS = S_INNER // S_HEADDIM
S_GROUPS = 4
S_STATE = 128
S_CONV = 4
S_BC = S_GROUPS * S_STATE
S_XBC = S_INNER + 2 * S_BC
D_FF = 2752
FFN_CONV = 3
IN_SIZES = (M_QK, M_QK, M_V, M_V, M_HEADS, M_HEADS, S_INNER, S_XBC, S_HEADS, D_MODEL, D_MODEL)
LN_EPS = 1e-5
RMS_EPS = 1e-6
NEG_BIG = -1e30
LOG2E = math.log2(math.e)

LANES = 128
SUBLANES = 8
D_FF_PAD = 2816
TILE = 256
TILE_GROUPS = TILE // SUBLANES
FFN_COLS = 256
FFN_BLOCKS = D_FF_PAD // FFN_COLS
FFN_SUBTILES = 4
MIX_SUBTILES = 1
VMEM_LIMIT = 56 * 1024 * 1024
MIX_CONV_COLS = 2 * M_QK + S_XBC
MIX_CONV_BLOCK = 512
MIX_PREFIX = (M_CONV - 1) * SUBLANES
FFN_PREFIX = (FFN_CONV - 1) * SUBLANES

OFF_Q = 0
OFF_K = OFF_Q + M_QK
OFF_V = OFF_K + M_QK
OFF_O = OFF_V + M_V
OFF_Z = OFF_O + M_V
OFF_XBC = OFF_Z + S_INNER
OFF_GA = OFF_XBC + S_XBC
OFF_GB = OFF_GA + D_MODEL
BIG_COLS = OFF_GB + D_MODEL
LANE_I = 0
LANE_F = LANE_I + M_HEADS
LANE_DT = LANE_F + M_HEADS


def _layer_norm(y, g, b):
    mu = jnp.mean(y, axis=-1, keepdims=True)
    yc = y - mu
    var = jnp.mean(yc * yc, axis=-1, keepdims=True)
    return yc * lax.rsqrt(var + LN_EPS) * g + b


def _sigmoid(x):
    return 0.5 * jnp.tanh(0.5 * x) + 0.5


def _silu(x):
    h = 0.5 * x
    return h + h * jnp.tanh(h)


def _token_index(n):
    p = jnp.arange(n, dtype=jnp.int32)
    return (p & (SUBLANES - 1)) * TILE_GROUPS + (p >> 3)


def _causal_mask():
    tok = _token_index(TILE)
    return jnp.where(tok[:, None] >= tok[None, :], 0.0, -jnp.inf).astype(F32)


def _token_scan(a, combine, identity, seed=None):
    row8 = lax.broadcasted_iota(jnp.int32, (SUBLANES, a.shape[1]), 0)
    run, cur = [], None
    for j in range(TILE_GROUPS):
        grp = a[j * SUBLANES:(j + 1) * SUBLANES, :]
        cur = grp if cur is None else combine(cur, grp)
        run.append(cur)
    before = jnp.where(row8 >= 1, pltpu.roll(run[-1], 1, axis=0), identity)
    for sh in (1, 2, 4):
        before = combine(before, jnp.where(row8 >= sh, pltpu.roll(before, sh, axis=0), identity))
    if seed is not None:
        before = combine(before, seed)
    return jnp.concatenate([combine(r, before) for r in run], axis=0)


def _fill_conv_prefix(stage, carry, width, c0, c1):
    n_pre = width - 1
    row8 = lax.broadcasted_iota(jnp.int32, (SUBLANES, c1 - c0), 0)
    for g in range(n_pre):
        r0 = TILE + g * SUBLANES
        cur = stage[r0:r0 + SUBLANES, c0:c1]
        prev = carry[g * SUBLANES:(g + 1) * SUBLANES, c0:c1]
        stage[g * SUBLANES:(g + 1) * SUBLANES, c0:c1] = jnp.where(
            row8 == 0, pltpu.roll(prev, 1, axis=0), pltpu.roll(cur, 1, axis=0))
        carry[g * SUBLANES:(g + 1) * SUBLANES, c0:c1] = cur


def _conv_taps(stage, w_ref, b_ref, width, c0, c1):
    p = (width - 1) * SUBLANES
    acc = b_ref[:, c0:c1] + w_ref[width - 1:width, c0:c1] * stage[p:p + TILE, c0:c1]
    for k in range(1, width):
        r0 = p - k * SUBLANES
        acc = acc + w_ref[width - 1 - k:width - k, c0:c1] * stage[r0:r0 + TILE, c0:c1]
    return acc


def _mixer_kernel(alpha, entry, n_in, *refs):
    x_ref, params = refs[0], refs[1:n_in]
    o_ref, scratch = refs[n_in], refs[n_in + 1:]
    _, car_ref, mst_ref, mm_ref, sst_ref = scratch[:5]

    @pl.when(pl.program_id(1) == 0)
    def _():
        car_ref[...] = jnp.zeros_like(car_ref)
        mst_ref[...] = jnp.zeros_like(mst_ref)
        sst_ref[...] = jnp.zeros_like(sst_ref)
        mm_ref[...] = jnp.full_like(mm_ref, NEG_BIG * LOG2E)

    for sub in range(MIX_SUBTILES):
        rows = pl.ds(sub * TILE, TILE)
        _mixer_tile(alpha, entry, x_ref.at[rows], *params, o_ref.at[rows], *scratch)


def _mixer_tile(alpha, entry, x_ref, ing_ref, inb_ref, wb_ref, ws_ref, mask_ref, cw_ref, cb_ref, gbias_ref, aneg_ref,
                dskip_ref, mng_ref, sng_ref, pa_ref, pbw_ref, wo_ref, lng_ref, lnb_ref, o_ref,
                stage_ref, car_ref, mst_ref, mm_ref, sst_ref, pv_ref, pe_ref,
                qb_ref, kb_ref, kt_ref, xs_ref, bm_ref, cm_ref, bt_ref, ha_ref, hb_ref):
    L = TILE
    qk_cols = 2 * M_QK

    def layer_input():
        x = x_ref[...]
        return _layer_norm(x, ing_ref[...], inb_ref[...]) if entry else x

    xb = layer_input().astype(BF16)

    def proj(c0, width):
        return jnp.dot(xb, wb_ref[:, c0:c0 + width], preferred_element_type=F32)

    def proj_extra(piece):
        src = (OFF_O, OFF_Z, OFF_GA, OFF_GB)[piece // 4] + (piece % 4) * 256
        pe_ref[:, piece * 256:(piece + 1) * 256] = proj(src, 256)

    ps_raw = jnp.dot(xb, ws_ref[...], preferred_element_type=F32)
    stage_ref[MIX_PREFIX:, :qk_cols] = proj(OFF_Q, qk_cols)
    _fill_conv_prefix(stage_ref, car_ref, M_CONV, 0, qk_cols)
    stage_ref[MIX_PREFIX:, qk_cols:] = proj(OFF_XBC, S_XBC)

    ps = ps_raw + gbias_ref[...]
    lane = lax.broadcasted_iota(jnp.int32, (L, LANES), 1)
    tail = jnp.log1p(jnp.exp(-jnp.abs(ps)))
    log_sig = jnp.minimum(ps, 0.0) - tail
    soft_plus = jnp.maximum(ps, 0.0) + tail
    is_f = (lane >= LANE_F) & (lane < LANE_DT)
    to_cum = jnp.where(is_f, log_sig, soft_plus * aneg_ref[...])
    cum = _token_scan(to_cum, jnp.add, 0.0)
    packed = jnp.where(lane < LANE_F, ps, cum) * LOG2E
    packed_t = packed.T

    w_all = pltpu.roll(packed, LANE_F - LANE_I, axis=1) - packed
    m_prev_all = mm_ref[...]
    mm_all = _token_scan(w_all, jnp.maximum, -jnp.inf, seed=m_prev_all)
    s_inter_all = jnp.exp2(m_prev_all[0:1, :] - mm_all)
    floor_all = jnp.exp2(-(packed + mm_all))
    mm_last_all = mm_all[L - 1:L, :]
    mm_ref[...] = jnp.broadcast_to(packed[L - 1:L, :] + mm_last_all, (SUBLANES, LANES))

    def lane_rep(a, c):
        return jnp.broadcast_to(a[:, c:c + 1], (L, LANES))

    mm_rep = [lane_rep(mm_all, LANE_F + h) for h in range(M_HEADS)]
    s_inter_rep = [lane_rep(s_inter_all, LANE_F + h) for h in range(M_HEADS)]
    floor_rep = [lane_rep(floor_all, LANE_F + h) for h in range(M_HEADS)]
    acs_rep = [lane_rep(packed, LANE_DT + hh) for hh in range(S_HEADS)]
    dt_rep = [lane_rep(soft_plus, LANE_DT + hh) for hh in range(S_HEADS)]

    def conv_block(i):
        return _silu(_conv_taps(stage_ref, cw_ref, cb_ref, M_CONV, i * MIX_CONV_BLOCK, (i + 1) * MIX_CONV_BLOCK))

    pv_ref[...] = proj(OFF_V, M_V).astype(BF16)
    qb_ref[...] = conv_block(0).astype(BF16)
    k_all = conv_block(1) * (M_DQK ** -0.5)
    kb_ref[...] = k_all.astype(BF16)
    kt_ref[...] = k_all.T
    _fill_conv_prefix(stage_ref, car_ref, S_CONV, qk_cols, MIX_CONV_COLS)
    proj_extra(0)
    xs_ref[:, :MIX_CONV_BLOCK] = conv_block(2)
    proj_extra(1)
    xs_ref[:, MIX_CONV_BLOCK:] = conv_block(3)
    proj_extra(2)
    b_all = conv_block(4)
    bm_ref[...] = b_all.astype(BF16)
    bt_ref[...] = b_all.T.astype(BF16)
    proj_extra(3)
    cm_ref[...] = conv_block(5).astype(BF16)

    lane_pair = lax.broadcasted_iota(jnp.int32, (L, 2 * M_DQK), 1)
    ones_blk = jnp.ones((L, M_DV), BF16)
    state_pairs = [mst_ref[p] for p in range(M_HEADS // 2)]
    for h in range(M_HEADS):
        pair, half = divmod(h, 2)
        c = LANE_F + h
        w_row = packed_t[LANE_I + h:LANE_I + h + 1, :] - packed_t[c:c + 1, :]
        mm_2 = jnp.concatenate([mm_rep[h], mm_rep[h]], axis=1)
        decay = jnp.exp2((w_row + mask_ref[...]) - mm_2)
        in_head = (lane_pair >= half * M_DQK) & (lane_pair < (half + 1) * M_DQK)
        q_pair = qb_ref[:, pair * 2 * M_DQK:(pair + 1) * 2 * M_DQK]
        k_pair = kb_ref[:, pair * 2 * M_DQK:(pair + 1) * 2 * M_DQK]
        q_h = jnp.where(in_head, q_pair, jnp.zeros_like(q_pair))
        scores = lax.dot_general(q_h, k_pair, (((1,), (1,)), ((), ())), preferred_element_type=F32) * decay
        proj_extra(4 + h)
        v_ext = jnp.concatenate([pv_ref[:, h * M_DV:(h + 1) * M_DV], ones_blk], axis=1)
        state_pair = state_pairs[pair]
        s_2 = jnp.concatenate([s_inter_rep[h], s_inter_rep[h]], axis=1)
        num_den = (jnp.dot(scores.astype(BF16), v_ext, preferred_element_type=F32)
                   + s_2 * jnp.dot(q_h, state_pair.astype(BF16), preferred_element_type=F32))
        kw_t = (kt_ref[h * M_DQK:(h + 1) * M_DQK, :]
                * jnp.exp2(w_row - mm_last_all[:, c:c + 1])).astype(BF16)
        mst_ref[pair, half * M_DQK:(half + 1) * M_DQK, :] = (
            s_inter_all[L - 1:L, c:c + 1] * state_pair[half * M_DQK:(half + 1) * M_DQK, :]
            + jnp.dot(kw_t, v_ext, preferred_element_type=F32))
        num = num_den[:, :M_DV]
        den = num_den[:, M_DV:]
        h_out = num / jnp.maximum(jnp.abs(den), floor_rep[h])
        ms = jnp.mean(h_out * h_out, axis=1, keepdims=True)
        h_n = h_out * lax.rsqrt(ms + RMS_EPS) * mng_ref[:, h * M_DV:(h + 1) * M_DV]
        o_gate = _sigmoid(pe_ref[:, h * M_DV:(h + 1) * M_DV])
        ha_ref[:, h * M_DV:(h + 1) * M_DV] = (o_gate * h_n).astype(BF16)

    ssd_states = [sst_ref[g] for g in range(S_GROUPS)]
    gw = S_INNER // S_GROUPS
    hpg = S_HEADS // S_GROUPS
    lane_g = lax.broadcasted_iota(jnp.int32, (L, gw), 1)
    low_half = lax.broadcasted_iota(jnp.int32, (L, LANES), 1) < S_HEADDIM
    assert hpg == 4 and 2 * S_HEADDIM == LANES

    for g in range(S_GROUPS):
        xs_g = xs_ref[:, g * gw:(g + 1) * gw]
        bm_g = bm_ref[:, g * S_STATE:(g + 1) * S_STATE]
        cm_g = cm_ref[:, g * S_STATE:(g + 1) * S_STATE]
        proj_extra(12 + g)
        cb = lax.dot_general(cm_g, bm_g, (((1,), (1,)), ((), ())), preferred_element_type=F32)
        y_off = jnp.dot(cm_g, ssd_states[g].astype(BF16), preferred_element_type=F32)
        h0 = g * hpg
        dt_g = jnp.concatenate([jnp.where(low_half, dt_rep[h0], dt_rep[h0 + 1]),
                                jnp.where(low_half, dt_rep[h0 + 2], dt_rep[h0 + 3])], axis=1)
        acs_g = jnp.concatenate([jnp.where(low_half, acs_rep[h0], acs_rep[h0 + 1]),
                                 jnp.where(low_half, acs_rep[h0 + 2], acs_rep[h0 + 3])], axis=1)
        last_g = acs_g[L - 1:L, :]
        xdt_g = xs_g * dt_g
        xdt_b = xdt_g.astype(BF16)
        y_diag = jnp.zeros((L, gw), F32)
        for r in range(hpg):
            c = LANE_DT + h0 + r
            acs_2 = jnp.concatenate([acs_rep[h0 + r], acs_rep[h0 + r]], axis=1)
            seg = jnp.exp2((acs_2 - packed_t[c:c + 1, :]) + mask_ref[...])
            mix = (cb * seg).astype(BF16)
            sel = (lane_g >= r * S_HEADDIM) & (lane_g < (r + 1) * S_HEADDIM)
            y_diag = y_diag + jnp.dot(mix, jnp.where(sel, xdt_b, jnp.zeros_like(xdt_b)),
                                      preferred_element_type=F32)
        y_g = y_diag + y_off * jnp.exp2(acs_g) + dskip_ref[:, g * gw:(g + 1) * gw] * xs_g
        x_dec = (xdt_g * jnp.exp2(last_g - acs_g)).astype(BF16)
        sst_ref[g] = (ssd_states[g] * jnp.exp2(last_g)
                      + jnp.dot(bt_ref[g * S_STATE:(g + 1) * S_STATE, :], x_dec, preferred_element_type=F32))
        gz = y_g * _silu(pe_ref[:, D_MODEL + g * gw:D_MODEL + (g + 1) * gw])
        ms = jnp.mean(gz * gz, axis=1, keepdims=True)
        hb_ref[:, g * gw:(g + 1) * gw] = (gz * lax.rsqrt(ms + RMS_EPS)
                                          * sng_ref[:, g * gw:(g + 1) * gw]).astype(BF16)

    gate_a = _sigmoid(pe_ref[:, 2 * D_MODEL:3 * D_MODEL])
    gate_b = _sigmoid(pe_ref[:, 3 * D_MODEL:])
    merged = (gate_a * jnp.dot(ha_ref[...], pa_ref[...], preferred_element_type=F32)
              + gate_b * jnp.dot(hb_ref[...], pbw_ref[...], preferred_element_type=F32))
    y = alpha * layer_input() + jnp.dot(merged.astype(BF16), wo_ref[...], preferred_element_type=F32)
    o_ref[...] = _layer_norm(y, lng_ref[...], lnb_ref[...])


def _mixer(alpha, entry, x2, in_ln_g, in_ln_b, prm, batch, seq):
    L = TILE
    rows = MIX_SUBTILES * L
    nt = seq // rows
    row = lambda b, t: (b * nt + t, 0)
    const = lambda b, t: (0, 0)

    def full(a):
        return pl.BlockSpec(a.shape, const, pipeline_mode=pl.Buffered(1))

    args = (x2, in_ln_g, in_ln_b, prm["w_big"], prm["w_small"], _causal_mask(), prm["cw"], prm["cb"],
            prm["gbias"], prm["aneg"], prm["dskip"], prm["mng"], prm["sng"], prm["pa"], prm["pb"], prm["wo"],
            prm["ln1g"], prm["ln1b"])
    in_specs = [pl.BlockSpec((rows, D_MODEL), row)] + [full(a) for a in args[1:]]
    return pl.pallas_call(
        functools.partial(_mixer_kernel, alpha, entry, len(args)),
        grid=(batch, nt),
        in_specs=in_specs,
        out_specs=pl.BlockSpec((rows, D_MODEL), row),
        out_shape=jax.ShapeDtypeStruct((batch * seq, D_MODEL), F32),
        scratch_shapes=[
            pltpu.VMEM((MIX_PREFIX + L, MIX_CONV_COLS), F32),
            pltpu.VMEM((MIX_PREFIX, MIX_CONV_COLS), F32),
            pltpu.VMEM((M_HEADS // 2, 2 * M_DQK, 2 * M_DV), F32),
            pltpu.VMEM((M_HEADS, LANES), F32),
            pltpu.VMEM((S_GROUPS, S_STATE, S_INNER // S_GROUPS), F32),
            pltpu.VMEM((L, M_V), BF16),
            pltpu.VMEM((L, 4 * D_MODEL), F32),
            pltpu.VMEM((L, M_QK), BF16),
            pltpu.VMEM((L, M_QK), BF16),
            pltpu.VMEM((M_QK, L), F32),
            pltpu.VMEM((L, S_INNER), F32),
            pltpu.VMEM((L, S_BC), BF16),
            pltpu.VMEM((L, S_BC), BF16),
            pltpu.VMEM((S_BC, L), BF16),
            pltpu.VMEM((L, M_V), BF16),
            pltpu.VMEM((L, S_INNER), BF16),
        ],
        compiler_params=pltpu.CompilerParams(
            dimension_semantics=("parallel", "arbitrary"), vmem_limit_bytes=VMEM_LIMIT),
        name="mixer",
    )(*args)


def _ffn_kernel(alpha, x_ref, wg_ref, wv_ref, cwg_ref, cwv_ref, cbg_ref, cbv_ref, wd_ref, lng_ref, lnb_ref,
                o_ref, stage_ref, car_ref, acc_ref):
    bw = 2 * FFN_COLS

    @pl.when(pl.program_id(1) == 0)
    def _():
        car_ref[...] = jnp.zeros_like(car_ref)

    def tile_rows(sub):
        return slice(sub * TILE, (sub + 1) * TILE)

    def up_proj(xb, j):
        cols = slice(j * FFN_COLS, (j + 1) * FFN_COLS)
        stage_ref[j % 2, FFN_PREFIX:, :FFN_COLS] = jnp.dot(xb, wg_ref[:, cols], preferred_element_type=F32)
        stage_ref[j % 2, FFN_PREFIX:, FFN_COLS:] = jnp.dot(xb, wv_ref[:, cols], preferred_element_type=F32)

    xb = x_ref[tile_rows(0), :].astype(BF16)
    up_proj(xb, 0)
    for sub in range(FFN_SUBTILES):
        for j in range(FFN_BLOCKS):
            if j + 1 < FFN_BLOCKS:
                up_proj(xb, j + 1)
            stage = stage_ref.at[j % 2]
            cols = slice(j * FFN_COLS, (j + 1) * FFN_COLS)
            _fill_conv_prefix(stage, car_ref.at[j], FFN_CONV, 0, bw)
            conv_g = _conv_taps(stage.at[:, :FFN_COLS], cwg_ref.at[:, cols], cbg_ref.at[:, cols],
                                FFN_CONV, 0, FFN_COLS)
            conv_v = _conv_taps(stage.at[:, FFN_COLS:], cwv_ref.at[:, cols], cbv_ref.at[:, cols],
                                FFN_CONV, 0, FFN_COLS)
            act = (_silu(conv_g) * conv_v).astype(BF16)
            part = jnp.dot(act, wd_ref[cols, :], preferred_element_type=F32)
            if j == 0:
                acc_ref[sub] = part
            else:
                acc_ref[sub] += part
        if sub + 1 < FFN_SUBTILES:
            xb = x_ref[tile_rows(sub + 1), :].astype(BF16)
            up_proj(xb, 0)
        y = alpha * x_ref[tile_rows(sub), :] + acc_ref[sub]
        o_ref[tile_rows(sub), :] = _layer_norm(y, lng_ref[...], lnb_ref[...])


def _ffn(alpha, x2, prm, batch, seq):
    rows = FFN_SUBTILES * TILE
    nt = seq // rows
    row = lambda b, t: (b * nt + t, 0)
    const = lambda b, t: (0, 0)

    def full(a):
        return pl.BlockSpec(a.shape, const)

    args = (x2, prm["wg"], prm["wv"], prm["cwg"], prm["cwv"], prm["cbg"], prm["cbv"], prm["wd"],
            prm["ln2g"], prm["ln2b"])
    return pl.pallas_call(
        functools.partial(_ffn_kernel, alpha),
        grid=(batch, nt),
        in_specs=[pl.BlockSpec((rows, D_MODEL), row)] + [full(a) for a in args[1:]],
        out_specs=pl.BlockSpec((rows, D_MODEL), row),
        out_shape=jax.ShapeDtypeStruct((batch * seq, D_MODEL), F32),
        scratch_shapes=[pltpu.VMEM((2, FFN_PREFIX + TILE, 2 * FFN_COLS), F32),
                        pltpu.VMEM((FFN_BLOCKS, FFN_PREFIX, 2 * FFN_COLS), F32),
                        pltpu.VMEM((FFN_SUBTILES, TILE, D_MODEL), F32)],
        compiler_params=pltpu.CompilerParams(
            dimension_semantics=("parallel", "arbitrary"), vmem_limit_bytes=VMEM_LIMIT),
        name="conv_ffn",
    )(*args)


REGROUP_ROWS = 128
W_IN_COLS = sum(IN_SIZES)
W_IN_COLS_PAD = -(-W_IN_COLS // LANES) * LANES


def _regroup_kernel(w_ref, ob_ref, os_ref):
    offs = [0]
    for s in IN_SIZES:
        offs.append(offs[-1] + s)

    def piece(first, last):
        c0, c1 = offs[first], offs[last]
        a0 = c0 // LANES * LANES
        a1 = -(-c1 // LANES) * LANES
        return w_ref[:, a0:a1][:, c0 - a0:c1 - a0].astype(BF16)

    ob_ref[:, OFF_Q:OFF_Z] = piece(0, 4)
    ob_ref[:, OFF_Z:OFF_GA] = piece(6, 8)
    ob_ref[:, OFF_GA:] = piece(9, 11)
    gates = jnp.concatenate([piece(4, 6), piece(8, 9)], axis=1)
    os_ref[...] = jnp.concatenate(
        [gates, jnp.zeros((REGROUP_ROWS, LANES - gates.shape[1]), BF16)], axis=1)


def _regroup_w_in(w_in):
    depth = w_in.shape[0]
    assert w_in.shape[1:] == (D_MODEL, W_IN_COLS) and D_MODEL % REGROUP_ROWS == 0
    return pl.pallas_call(
        _regroup_kernel,
        grid=(depth, D_MODEL // REGROUP_ROWS),
        in_specs=[pl.BlockSpec((None, REGROUP_ROWS, W_IN_COLS_PAD), lambda l, i: (l, i, 0))],
        out_specs=[pl.BlockSpec((None, REGROUP_ROWS, BIG_COLS), lambda l, i: (l, i, 0)),
                   pl.BlockSpec((None, REGROUP_ROWS, LANES), lambda l, i: (l, i, 0))],
        out_shape=[jax.ShapeDtypeStruct((depth, D_MODEL, BIG_COLS), BF16),
                   jax.ShapeDtypeStruct((depth, D_MODEL, LANES), BF16)],
        compiler_params=pltpu.CompilerParams(dimension_semantics=("parallel", "parallel")),
        name="regroup_w_in",
    )(w_in)
def _pad_ff(a, dtype):
    return jnp.pad(a.astype(dtype), ((0, 0), (0, D_FF_PAD - D_FF)))


def _layer_params(l, w_big, w_small, m_conv_w, m_conv_b, m_i_bias, m_f_bias, m_norm_g, s_conv_w, s_conv_b, s_dt_bias,
                  s_a_log, s_d, s_norm_g, p_a, p_b, w_out, ln1_g, ln1_b, w_up, f_conv_w, f_conv_b, w_down,
                  ln2_g, ln2_b):
    row = lambda a: a.reshape(1, -1).astype(F32)
    pad_small = LANES - 2 * M_HEADS - S_HEADS
    fcb = row(f_conv_b[l])
    prm = {
        "w_big": w_big[l], "w_small": w_small[l],
        "cw": jnp.concatenate([m_conv_w[l], s_conv_w[l]], axis=1).astype(F32),
        "cb": jnp.concatenate([row(m_conv_b[l]), row(s_conv_b[l])], axis=1),
        "gbias": jnp.pad(jnp.concatenate([m_i_bias[l], m_f_bias[l], s_dt_bias[l]]).astype(F32),
                         (0, pad_small)).reshape(1, LANES),
        "aneg": jnp.pad(-jnp.exp(s_a_log[l].astype(F32)), (LANE_DT, pad_small)).reshape(1, LANES),
        "dskip": jnp.repeat(s_d[l].astype(F32), S_HEADDIM).reshape(1, S_INNER),
        "mng": row(m_norm_g[l]), "sng": row(s_norm_g[l]),
        "pa": p_a[l].astype(BF16), "pb": p_b[l].astype(BF16), "wo": w_out[l].astype(BF16),
        "ln1g": row(ln1_g[l]), "ln1b": row(ln1_b[l]),
        "wg": _pad_ff(w_up[l][:, :D_FF], BF16), "wv": _pad_ff(w_up[l][:, D_FF:], BF16),
        "cwg": _pad_ff(f_conv_w[l][:, :D_FF], F32), "cwv": _pad_ff(f_conv_w[l][:, D_FF:], F32),
        "cbg": _pad_ff(fcb[:, :D_FF], F32), "cbv": _pad_ff(fcb[:, D_FF:], F32),
        "wd": jnp.pad(w_down[l].astype(BF16), ((0, D_FF_PAD - D_FF), (0, 0))),
        "ln2g": row(ln2_g[l]), "ln2b": row(ln2_b[l]),
    }
    return prm


def _to_tile_order(x, batch, seq):
    nt = seq // TILE
    x5 = x.reshape(batch, nt, SUBLANES, TILE_GROUPS, D_MODEL)
    return jnp.transpose(x5, (0, 1, 3, 2, 4)).reshape(batch * seq, D_MODEL)


def _from_tile_order(x2, batch, seq):
    nt = seq // TILE
    x5 = x2.reshape(batch, nt, TILE_GROUPS, SUBLANES, D_MODEL)
    return jnp.transpose(x5, (0, 1, 3, 2, 4)).reshape(batch, seq, D_MODEL)


def kernel(x, in_ln_g, in_ln_b, w_in, m_conv_w, m_conv_b, m_i_bias, m_f_bias, m_norm_g, s_conv_w, s_conv_b, s_dt_bias, s_a_log, s_d, s_norm_g, p_a, p_b, w_out, ln1_g, ln1_b, w_up, f_conv_w, f_conv_b, w_down, ln2_g, ln2_b):
    batch, seq, d_model = x.shape
    depth = w_in.shape[0]
    assert d_model == D_MODEL and w_in.shape[2] == sum(IN_SIZES)
    assert seq % (FFN_SUBTILES * TILE) == 0 and seq % (MIX_SUBTILES * TILE) == 0 and depth >= 1
    alpha = float((2 * depth) ** 0.25)
    x2 = _to_tile_order(x.astype(F32), batch, seq)
    in_g, in_b = in_ln_g.reshape(1, -1).astype(F32), in_ln_b.reshape(1, -1).astype(F32)
    w_big, w_small = _regroup_w_in(w_in.astype(F32))
    for l in range(depth):
        prm = _layer_params(l, w_big, w_small, m_conv_w, m_conv_b, m_i_bias, m_f_bias, m_norm_g, s_conv_w, s_conv_b,
                            s_dt_bias, s_a_log, s_d, s_norm_g, p_a, p_b, w_out, ln1_g, ln1_b, w_up, f_conv_w,
                            f_conv_b, w_down, ln2_g, ln2_b)
        x2 = _mixer(alpha, l == 0, x2, in_g, in_b, prm, batch, seq)
        x2 = _ffn(alpha, x2, prm, batch, seq)
    return _from_tile_order(x2, batch, seq).astype(x.dtype)
```

```python
import functools
import math

import jax
import jax.numpy as jnp
from jax import lax
from jax.experimental import pallas as pl
from jax.experimental.pallas import tpu as pltpu

F32 = jnp.float32
BF16 = jnp.bfloat16

D_MODEL = 1024
M_HEADS = 8
M_DQK = 64
M_DV = 128
M_QK = M_HEADS * M_DQK
M_V = M_HEADS * M_DV
M_CONV = 4
S_HEADDIM = 64
S_INNER = D_MODEL
S_HEADS = S_INNER // S_HEADDIM
S_GROUPS = 4
S_STATE = 128
S_CONV = 4
S_BC = S_GROUPS * S_STATE
S_XBC = S_INNER + 2 * S_BC
D_FF = 2752
FFN_CONV = 3
IN_SIZES = (M_QK, M_QK, M_V, M_V, M_HEADS, M_HEADS, S_INNER, S_XBC, S_HEADS, D_MODEL, D_MODEL)
LN_EPS = 1e-5
RMS_EPS = 1e-6
NEG_BIG = -1e30
LOG2E = math.log2(math.e)

LANES = 128
SUBLANES = 8
D_FF_PAD = 2816
TILE = 256
TILE_GROUPS = TILE // SUBLANES
FFN_COLS = 256
FFN_BLOCKS = D_FF_PAD // FFN_COLS
FFN_SUBTILES = 4
MIX_SUBTILES = 1
VMEM_LIMIT = 56 * 1024 * 1024
MIX_CONV_COLS = 2 * M_QK + S_XBC
MIX_CONV_BLOCK = 512
MIX_PREFIX = (M_CONV - 1) * SUBLANES
FFN_PREFIX = (FFN_CONV - 1) * SUBLANES

OFF_Q = 0
OFF_K = OFF_Q + M_QK
OFF_V = OFF_K + M_QK
OFF_O = OFF_V + M_V
OFF_Z = OFF_O + M_V
OFF_XBC = OFF_Z + S_INNER
OFF_GA = OFF_XBC + S_XBC
OFF_GB = OFF_GA + D_MODEL
BIG_COLS = OFF_GB + D_MODEL
LANE_I = 0
LANE_F = LANE_I + M_HEADS
LANE_DT = LANE_F + M_HEADS


def _layer_norm(y, g, b):
    mu = jnp.mean(y, axis=-1, keepdims=True)
    yc = y - mu
    var = jnp.mean(yc * yc, axis=-1, keepdims=True)
    return yc * lax.rsqrt(var + LN_EPS) * g + b


def _sigmoid(x):
    return 0.5 * jnp.tanh(0.5 * x) + 0.5


def _silu(x):
    h = 0.5 * x
    return h + h * jnp.tanh(h)


def _token_index(n):
    p = jnp.arange(n, dtype=jnp.int32)
    return (p & (SUBLANES - 1)) * TILE_GROUPS + (p >> 3)


def _rows_to_tile_order(a):
    return jnp.swapaxes(a.reshape(SUBLANES, TILE_GROUPS, a.shape[1]), 0, 1).reshape(a.shape)


def _rows_from_tile_order(a):
    return jnp.swapaxes(a.reshape(TILE_GROUPS, SUBLANES, a.shape[1]), 0, 1).reshape(a.shape)


def _causal_mask():
    tok = _token_index(TILE)
    return jnp.where(tok[:, None] >= tok[None, :], 0.0, -jnp.inf).astype(F32)


def _token_scan(a, combine, identity, seed=None):
    row8 = lax.broadcasted_iota(jnp.int32, (SUBLANES, a.shape[1]), 0)
    run, cur = [], None
    for j in range(TILE_GROUPS):
        grp = a[j * SUBLANES:(j + 1) * SUBLANES, :]
        cur = grp if cur is None else combine(cur, grp)
        run.append(cur)
    before = jnp.where(row8 >= 1, pltpu.roll(run[-1], 1, axis=0), identity)
    for sh in (1, 2, 4):
        before = combine(before, jnp.where(row8 >= sh, pltpu.roll(before, sh, axis=0), identity))
    if seed is not None:
        before = combine(before, seed)
    return jnp.concatenate([combine(r, before) for r in run], axis=0)


def _fill_conv_prefix(stage, carry, width, c0, c1):
    n_pre = width - 1
    row8 = lax.broadcasted_iota(jnp.int32, (SUBLANES, c1 - c0), 0)
    for g in range(n_pre):
        r0 = TILE + g * SUBLANES
        cur = stage[r0:r0 + SUBLANES, c0:c1]
        prev = carry[g * SUBLANES:(g + 1) * SUBLANES, c0:c1]
        stage[g * SUBLANES:(g + 1) * SUBLANES, c0:c1] = jnp.where(
            row8 == 0, pltpu.roll(prev, 1, axis=0), pltpu.roll(cur, 1, axis=0))
        carry[g * SUBLANES:(g + 1) * SUBLANES, c0:c1] = cur


def _conv_taps(stage, w_ref, b_ref, width, c0, c1):
    p = (width - 1) * SUBLANES
    acc = b_ref[:, c0:c1] + w_ref[width - 1:width, c0:c1] * stage[p:p + TILE, c0:c1]
    for k in range(1, width):
        r0 = p - k * SUBLANES
        acc = acc + w_ref[width - 1 - k:width - k, c0:c1] * stage[r0:r0 + TILE, c0:c1]
    return acc


def _mixer_kernel(alpha, entry, n_in, *refs):
    x_ref, params = refs[0], refs[1:n_in]
    o_ref, scratch = refs[n_in], refs[n_in + 1:]
    _, car_ref, mst_ref, mm_ref, sst_ref = scratch[:5]

    @pl.when(pl.program_id(1) == 0)
    def _():
        car_ref[...] = jnp.zeros_like(car_ref)
        mst_ref[...] = jnp.zeros_like(mst_ref)
        sst_ref[...] = jnp.zeros_like(sst_ref)
        mm_ref[...] = jnp.full_like(mm_ref, NEG_BIG * LOG2E)

    for sub in range(MIX_SUBTILES):
        rows = pl.ds(sub * TILE, TILE)
        _mixer_tile(alpha, entry, x_ref.at[rows], *params, o_ref.at[rows], *scratch)


def _mixer_tile(alpha, entry, x_ref, ing_ref, inb_ref, wb_ref, ws_ref, mask_ref, cw_ref, cb_ref, gbias_ref, aneg_ref,
                dskip_ref, mng_ref, sng_ref, pa_ref, pbw_ref, wo_ref, lng_ref, lnb_ref, o_ref,
                stage_ref, car_ref, mst_ref, mm_ref, sst_ref, pv_ref, pe_ref,
                qb_ref, kb_ref, kt_ref, xs_ref, bm_ref, cm_ref, bt_ref, ha_ref, hb_ref, ma_ref, xin_ref):
    L = TILE
    qk_cols = 2 * M_QK

    if entry:
        xin_ref[...] = _layer_norm(_rows_to_tile_order(x_ref[...]), ing_ref[...], inb_ref[...])

    def layer_input():
        return xin_ref[...] if entry else x_ref[...]

    xb = layer_input().astype(BF16)

    def proj(c0, width):
        return jnp.dot(xb, wb_ref[:, c0:c0 + width], preferred_element_type=F32)

    def proj_extra(piece):
        src = (OFF_O, OFF_Z, OFF_GA, OFF_GB)[piece // 4] + (piece % 4) * 256
        pe_ref[:, piece * 256:(piece + 1) * 256] = proj(src, 256)

    ps_raw = jnp.dot(xb, ws_ref[...], preferred_element_type=F32)
    stage_ref[MIX_PREFIX:, :qk_cols] = proj(OFF_Q, qk_cols)
    _fill_conv_prefix(stage_ref, car_ref, M_CONV, 0, qk_cols)
    stage_ref[MIX_PREFIX:, qk_cols:] = proj(OFF_XBC, S_XBC)

    ps = ps_raw + gbias_ref[...]
    lane = lax.broadcasted_iota(jnp.int32, (L, LANES), 1)
    tail = jnp.log1p(jnp.exp(-jnp.abs(ps)))
    log_sig = jnp.minimum(ps, 0.0) - tail
    soft_plus = jnp.maximum(ps, 0.0) + tail
    is_f = (lane >= LANE_F) & (lane < LANE_DT)
    to_cum = jnp.where(is_f, log_sig, soft_plus * aneg_ref[...])
    cum = _token_scan(to_cum, jnp.add, 0.0)
    packed = jnp.where(lane < LANE_F, ps, cum) * LOG2E
    packed_t = packed.T

    w_all = pltpu.roll(packed, LANE_F - LANE_I, axis=1) - packed
    m_prev_all = mm_ref[...]
    mm_all = _token_scan(w_all, jnp.maximum, -jnp.inf, seed=m_prev_all)
    s_inter_all = jnp.exp2(m_prev_all[0:1, :] - mm_all)
    floor_all = jnp.exp2(-(packed + mm_all))
    mm_last_all = mm_all[L - 1:L, :]
    mm_ref[...] = jnp.broadcast_to(packed[L - 1:L, :] + mm_last_all, (SUBLANES, LANES))

    def lane_rep(a, c):
        return jnp.broadcast_to(a[:, c:c + 1], (L, LANES))

    mm_rep = [lane_rep(mm_all, LANE_F + h) for h in range(M_HEADS)]
    s_inter_rep = [lane_rep(s_inter_all, LANE_F + h) for h in range(M_HEADS)]
    floor_rep = [lane_rep(floor_all, LANE_F + h) for h in range(M_HEADS)]
    acs_rep = [lane_rep(packed, LANE_DT + hh) for hh in range(S_HEADS)]
    dt_rep = [lane_rep(soft_plus, LANE_DT + hh) for hh in range(S_HEADS)]

    def conv_block(i):
        return _silu(_conv_taps(stage_ref, cw_ref, cb_ref, M_CONV, i * MIX_CONV_BLOCK, (i + 1) * MIX_CONV_BLOCK))

    pv_ref[...] = proj(OFF_V, M_V).astype(BF16)
    qb_ref[...] = conv_block(0).astype(BF16)
    proj_extra(0)
    k_all = conv_block(1) * (M_DQK ** -0.5)
    kb_ref[...] = k_all.astype(BF16)
    kt_ref[...] = k_all.T
    _fill_conv_prefix(stage_ref, car_ref, S_CONV, qk_cols, MIX_CONV_COLS)
    proj_extra(1)
    proj_extra(2)
    xs_ref[:, :MIX_CONV_BLOCK] = conv_block(2)
    proj_extra(3)
    proj_extra(4)
    xs_ref[:, MIX_CONV_BLOCK:] = conv_block(3)
    proj_extra(5)
    b_all = conv_block(4)
    bm_ref[...] = b_all.astype(BF16)
    bt_ref[...] = b_all.T.astype(BF16)
    proj_extra(6)
    proj_extra(7)
    cm_ref[...] = conv_block(5).astype(BF16)

    lane_pair = lax.broadcasted_iota(jnp.int32, (L, 2 * M_DQK), 1)
    ones_blk = jnp.ones((L, M_DV), BF16)
    state_pairs = [mst_ref[p] for p in range(M_HEADS // 2)]
    for h in range(M_HEADS):
        pair, half = divmod(h, 2)
        c = LANE_F + h
        w_row = packed_t[LANE_I + h:LANE_I + h + 1, :] - packed_t[c:c + 1, :]
        mm_2 = jnp.concatenate([mm_rep[h], mm_rep[h]], axis=1)
        decay = jnp.exp2((w_row + mask_ref[...]) - mm_2)
        in_head = (lane_pair >= half * M_DQK) & (lane_pair < (half + 1) * M_DQK)
        q_pair = qb_ref[:, pair * 2 * M_DQK:(pair + 1) * 2 * M_DQK]
        k_pair = kb_ref[:, pair * 2 * M_DQK:(pair + 1) * 2 * M_DQK]
        q_h = jnp.where(in_head, q_pair, jnp.zeros_like(q_pair))
        scores = lax.dot_general(q_h, k_pair, (((1,), (1,)), ((), ())), preferred_element_type=F32) * decay
        if h % 2 == 0:
            proj_extra(8 + h // 2)
        v_ext = jnp.concatenate([pv_ref[:, h * M_DV:(h + 1) * M_DV], ones_blk], axis=1)
        state_pair = state_pairs[pair]
        s_2 = jnp.concatenate([s_inter_rep[h], s_inter_rep[h]], axis=1)
        num_den = (jnp.dot(scores.astype(BF16), v_ext, preferred_element_type=F32)
                   + s_2 * jnp.dot(q_h, state_pair.astype(BF16), preferred_element_type=F32))
        kw_t = (kt_ref[h * M_DQK:(h + 1) * M_DQK, :]
                * jnp.exp2(w_row - mm_last_all[:, c:c + 1])).astype(BF16)
        mst_ref[pair, half * M_DQK:(half + 1) * M_DQK, :] = (
            s_inter_all[L - 1:L, c:c + 1] * state_pair[half * M_DQK:(half + 1) * M_DQK, :]
            + jnp.dot(kw_t, v_ext, preferred_element_type=F32))
        num = num_den[:, :M_DV]
        den = num_den[:, M_DV:]
        h_out = num / jnp.maximum(jnp.abs(den), floor_rep[h])
        ms = jnp.mean(h_out * h_out, axis=1, keepdims=True)
        h_n = h_out * lax.rsqrt(ms + RMS_EPS) * mng_ref[:, h * M_DV:(h + 1) * M_DV]
        o_gate = _sigmoid(pe_ref[:, h * M_DV:(h + 1) * M_DV])
        ha_ref[:, h * M_DV:(h + 1) * M_DV] = (o_gate * h_n).astype(BF16)

    ssd_states = [sst_ref[g] for g in range(S_GROUPS)]
    gw = S_INNER // S_GROUPS
    hpg = S_HEADS // S_GROUPS
    lane_g = lax.broadcasted_iota(jnp.int32, (L, gw), 1)
    low_half = lax.broadcasted_iota(jnp.int32, (L, LANES), 1) < S_HEADDIM
    assert hpg == 4 and 2 * S_HEADDIM == LANES

    for g in range(S_GROUPS):
        xs_g = xs_ref[:, g * gw:(g + 1) * gw]
        bm_g = bm_ref[:, g * S_STATE:(g + 1) * S_STATE]
        cm_g = cm_ref[:, g * S_STATE:(g + 1) * S_STATE]
        proj_extra(12 + g)
        ma_ref[:, g * gw:(g + 1) * gw] = jnp.dot(ha_ref[...], pa_ref[:, g * gw:(g + 1) * gw],
                                                 preferred_element_type=F32)
        cb = lax.dot_general(cm_g, bm_g, (((1,), (1,)), ((), ())), preferred_element_type=F32)
        y_off = jnp.dot(cm_g, ssd_states[g].astype(BF16), preferred_element_type=F32)
        h0 = g * hpg
        dt_g = jnp.concatenate([jnp.where(low_half, dt_rep[h0], dt_rep[h0 + 1]),
                                jnp.where(low_half, dt_rep[h0 + 2], dt_rep[h0 + 3])], axis=1)
        acs_g = jnp.concatenate([jnp.where(low_half, acs_rep[h0], acs_rep[h0 + 1]),
                                 jnp.where(low_half, acs_rep[h0 + 2], acs_rep[h0 + 3])], axis=1)
        last_g = acs_g[L - 1:L, :]
        xdt_g = xs_g * dt_g
        xdt_b = xdt_g.astype(BF16)
        y_diag = jnp.zeros((L, gw), F32)
        for r in range(hpg):
            c = LANE_DT + h0 + r
            acs_2 = jnp.concatenate([acs_rep[h0 + r], acs_rep[h0 + r]], axis=1)
            seg = jnp.exp2((acs_2 - packed_t[c:c + 1, :]) + mask_ref[...])
            mix = (cb * seg).astype(BF16)
            sel = (lane_g >= r * S_HEADDIM) & (lane_g < (r + 1) * S_HEADDIM)
            y_diag = y_diag + jnp.dot(mix, jnp.where(sel, xdt_b, jnp.zeros_like(xdt_b)),
                                      preferred_element_type=F32)
        y_g = y_diag + y_off * jnp.exp2(acs_g) + dskip_ref[:, g * gw:(g + 1) * gw] * xs_g
        x_dec = (xdt_g * jnp.exp2(last_g - acs_g)).astype(BF16)
        sst_ref[g] = (ssd_states[g] * jnp.exp2(last_g)
                      + jnp.dot(bt_ref[g * S_STATE:(g + 1) * S_STATE, :], x_dec, preferred_element_type=F32))
        gz = y_g * _silu(pe_ref[:, D_MODEL + g * gw:D_MODEL + (g + 1) * gw])
        ms = jnp.mean(gz * gz, axis=1, keepdims=True)
        hb_ref[:, g * gw:(g + 1) * gw] = (gz * lax.rsqrt(ms + RMS_EPS)
                                          * sng_ref[:, g * gw:(g + 1) * gw]).astype(BF16)

    gate_a = _sigmoid(pe_ref[:, 2 * D_MODEL:3 * D_MODEL])
    gate_b = _sigmoid(pe_ref[:, 3 * D_MODEL:])
    merged = gate_a * ma_ref[...] + gate_b * jnp.dot(hb_ref[...], pbw_ref[...], preferred_element_type=F32)
    y = alpha * layer_input() + jnp.dot(merged.astype(BF16), wo_ref[...], preferred_element_type=F32)
    o_ref[...] = _layer_norm(y, lng_ref[...], lnb_ref[...])


def _mixer(alpha, entry, x2, in_ln_g, in_ln_b, prm, batch, seq):
    L = TILE
    rows = MIX_SUBTILES * L
    nt = seq // rows
    row = lambda b, t: (b * nt + t, 0)
    const = lambda b, t: (0, 0)

    def full(a):
        return pl.BlockSpec(a.shape, const, pipeline_mode=pl.Buffered(1))

    args = (x2, in_ln_g, in_ln_b, prm["w_big"], prm["w_small"], _causal_mask(), prm["cw"], prm["cb"],
            prm["gbias"], prm["aneg"], prm["dskip"], prm["mng"], prm["sng"], prm["pa"], prm["pb"], prm["wo"],
            prm["ln1g"], prm["ln1b"])
    in_specs = [pl.BlockSpec((rows, D_MODEL), row)] + [full(a) for a in args[1:]]
    return pl.pallas_call(
        functools.partial(_mixer_kernel, alpha, entry, len(args)),
        grid=(batch, nt),
        in_specs=in_specs,
        out_specs=pl.BlockSpec((rows, D_MODEL), row),
        out_shape=jax.ShapeDtypeStruct((batch * seq, D_MODEL), F32),
        scratch_shapes=[
            pltpu.VMEM((MIX_PREFIX + L, MIX_CONV_COLS), F32),
            pltpu.VMEM((MIX_PREFIX, MIX_CONV_COLS), F32),
            pltpu.VMEM((M_HEADS // 2, 2 * M_DQK, 2 * M_DV), F32),
            pltpu.VMEM((M_HEADS, LANES), F32),
            pltpu.VMEM((S_GROUPS, S_STATE, S_INNER // S_GROUPS), F32),
            pltpu.VMEM((L, M_V), BF16),
            pltpu.VMEM((L, 4 * D_MODEL), F32),
            pltpu.VMEM((L, M_QK), BF16),
            pltpu.VMEM((L, M_QK), BF16),
            pltpu.VMEM((M_QK, L), F32),
            pltpu.VMEM((L, S_INNER), F32),
            pltpu.VMEM((L, S_BC), BF16),
            pltpu.VMEM((L, S_BC), BF16),
            pltpu.VMEM((S_BC, L), BF16),
            pltpu.VMEM((L, M_V), BF16),
            pltpu.VMEM((L, S_INNER), BF16),
            pltpu.VMEM((L, D_MODEL), F32),
            pltpu.VMEM((L, D_MODEL), F32),
        ],
        compiler_params=pltpu.CompilerParams(
            dimension_semantics=("parallel", "arbitrary"), vmem_limit_bytes=VMEM_LIMIT),
        name="mixer",
    )(*args)


def _ffn_kernel(alpha, exit_, x_ref, wg_ref, wv_ref, cwg_ref, cwv_ref, cbg_ref, cbv_ref, wd_ref, lng_ref, lnb_ref,
                o_ref, stage_ref, car_ref, acc_ref):
    bw = 2 * FFN_COLS

    @pl.when(pl.program_id(1) == 0)
    def _():
        car_ref[...] = jnp.zeros_like(car_ref)

    def tile_rows(sub):
        return slice(sub * TILE, (sub + 1) * TILE)

    def up_proj(xb, j):
        cols = slice(j * FFN_COLS, (j + 1) * FFN_COLS)
        stage_ref[j % 2, FFN_PREFIX:, :FFN_COLS] = jnp.dot(xb, wg_ref[:, cols], preferred_element_type=F32)
        stage_ref[j % 2, FFN_PREFIX:, FFN_COLS:] = jnp.dot(xb, wv_ref[:, cols], preferred_element_type=F32)

    xb = x_ref[tile_rows(0), :].astype(BF16)
    up_proj(xb, 0)
    for sub in range(FFN_SUBTILES):
        for j in range(FFN_BLOCKS):
            if j + 1 < FFN_BLOCKS:
                up_proj(xb, j + 1)
            stage = stage_ref.at[j % 2]
            cols = slice(j * FFN_COLS, (j + 1) * FFN_COLS)
            _fill_conv_prefix(stage, car_ref.at[j], FFN_CONV, 0, bw)
            conv_g = _conv_taps(stage.at[:, :FFN_COLS], cwg_ref.at[:, cols], cbg_ref.at[:, cols],
                                FFN_CONV, 0, FFN_COLS)
            conv_v = _conv_taps(stage.at[:, FFN_COLS:], cwv_ref.at[:, cols], cbv_ref.at[:, cols],
                                FFN_CONV, 0, FFN_COLS)
            act = (_silu(conv_g) * conv_v).astype(BF16)
            part = jnp.dot(act, wd_ref[cols, :], preferred_element_type=F32)
            if j == 0:
                acc_ref[sub] = part
            else:
                acc_ref[sub] += part
        if sub + 1 < FFN_SUBTILES:
            xb = x_ref[tile_rows(sub + 1), :].astype(BF16)
            up_proj(xb, 0)
        y = alpha * x_ref[tile_rows(sub), :] + acc_ref[sub]
        out = _layer_norm(y, lng_ref[...], lnb_ref[...])
        o_ref[tile_rows(sub), :] = _rows_from_tile_order(out) if exit_ else out


def _ffn(alpha, exit_, x2, prm, batch, seq):
    rows = FFN_SUBTILES * TILE
    nt = seq // rows
    row = lambda b, t: (b * nt + t, 0)
    const = lambda b, t: (0, 0)

    def full(a):
        return pl.BlockSpec(a.shape, const)

    args = (x2, prm["wg"], prm["wv"], prm["cwg"], prm["cwv"], prm["cbg"], prm["cbv"], prm["wd"],
            prm["ln2g"], prm["ln2b"])
    return pl.pallas_call(
        functools.partial(_ffn_kernel, alpha, exit_),
        grid=(batch, nt),
        in_specs=[pl.BlockSpec((rows, D_MODEL), row)] + [full(a) for a in args[1:]],
        out_specs=pl.BlockSpec((rows, D_MODEL), row),
        out_shape=jax.ShapeDtypeStruct((batch * seq, D_MODEL), F32),
        scratch_shapes=[pltpu.VMEM((2, FFN_PREFIX + TILE, 2 * FFN_COLS), F32),
                        pltpu.VMEM((FFN_BLOCKS, FFN_PREFIX, 2 * FFN_COLS), F32),
                        pltpu.VMEM((FFN_SUBTILES, TILE, D_MODEL), F32)],
        compiler_params=pltpu.CompilerParams(
            dimension_semantics=("parallel", "arbitrary"), vmem_limit_bytes=VMEM_LIMIT),
        name="conv_ffn",
    )(*args)


def _pad_ff(a, dtype):
    return jnp.pad(a.astype(dtype), ((0, 0), (0, D_FF_PAD - D_FF)))


def _layer_params(l, w_in, m_conv_w, m_conv_b, m_i_bias, m_f_bias, m_norm_g, s_conv_w, s_conv_b, s_dt_bias,
                  s_a_log, s_d, s_norm_g, p_a, p_b, w_out, ln1_g, ln1_b, w_up, f_conv_w, f_conv_b, w_down,
                  ln2_g, ln2_b):
    offs = [0]
    for s in IN_SIZES:
        offs.append(offs[-1] + s)
    w = w_in[l]
    part = [w[:, offs[i]:offs[i + 1]] for i in range(len(IN_SIZES))]
    wq, wk, wv, wo_, wi, wf, wz, wxbc, wdt, wga, wgb = part
    row = lambda a: a.reshape(1, -1).astype(F32)
    pad_small = LANES - 2 * M_HEADS - S_HEADS
    fcb = row(f_conv_b[l])
    prm = {
        "w_big": jnp.concatenate([wq, wk, wv, wo_, wz, wxbc, wga, wgb], axis=1).astype(BF16),
        "w_small": jnp.pad(jnp.concatenate([wi, wf, wdt], axis=1), ((0, 0), (0, pad_small))).astype(BF16),
        "cw": jnp.concatenate([m_conv_w[l], s_conv_w[l]], axis=1).astype(F32),
        "cb": jnp.concatenate([row(m_conv_b[l]), row(s_conv_b[l])], axis=1),
        "gbias": jnp.pad(jnp.concatenate([m_i_bias[l], m_f_bias[l], s_dt_bias[l]]).astype(F32),
                         (0, pad_small)).reshape(1, LANES),
        "aneg": jnp.pad(-jnp.exp(s_a_log[l].astype(F32)), (LANE_DT, pad_small)).reshape(1, LANES),
        "dskip": jnp.repeat(s_d[l].astype(F32), S_HEADDIM).reshape(1, S_INNER),
        "mng": row(m_norm_g[l]), "sng": row(s_norm_g[l]),
        "pa": p_a[l].astype(BF16), "pb": p_b[l].astype(BF16), "wo": w_out[l].astype(BF16),
        "ln1g": row(ln1_g[l]), "ln1b": row(ln1_b[l]),
        "wg": _pad_ff(w_up[l][:, :D_FF], BF16), "wv": _pad_ff(w_up[l][:, D_FF:], BF16),
        "cwg": _pad_ff(f_conv_w[l][:, :D_FF], F32), "cwv": _pad_ff(f_conv_w[l][:, D_FF:], F32),
        "cbg": _pad_ff(fcb[:, :D_FF], F32), "cbv": _pad_ff(fcb[:, D_FF:], F32),
        "wd": jnp.pad(w_down[l].astype(BF16), ((0, D_FF_PAD - D_FF), (0, 0))),
        "ln2g": row(ln2_g[l]), "ln2b": row(ln2_b[l]),
    }
    return prm


def kernel(x, in_ln_g, in_ln_b, w_in, m_conv_w, m_conv_b, m_i_bias, m_f_bias, m_norm_g, s_conv_w, s_conv_b, s_dt_bias, s_a_log, s_d, s_norm_g, p_a, p_b, w_out, ln1_g, ln1_b, w_up, f_conv_w, f_conv_b, w_down, ln2_g, ln2_b):
    batch, seq, d_model = x.shape
    depth = w_in.shape[0]
    assert d_model == D_MODEL and w_in.shape[2] == sum(IN_SIZES)
    assert seq % (FFN_SUBTILES * TILE) == 0 and seq % (MIX_SUBTILES * TILE) == 0 and depth >= 1
    alpha = float((2 * depth) ** 0.25)
    x2 = x.astype(F32).reshape(batch * seq, D_MODEL)
    in_g, in_b = in_ln_g.reshape(1, -1).astype(F32), in_ln_b.reshape(1, -1).astype(F32)
    for l in range(depth):
        prm = _layer_params(l, w_in, m_conv_w, m_conv_b, m_i_bias, m_f_bias, m_norm_g, s_conv_w, s_conv_b,
                            s_dt_bias, s_a_log, s_d, s_norm_g, p_a, p_b, w_out, ln1_g, ln1_b, w_up, f_conv_w,
                            f_conv_b, w_down, ln2_g, ln2_b)
        x2 = _mixer(alpha, l == 0, x2, in_g, in_b, prm, batch, seq)
        x2 = _ffn(alpha, l == depth - 1, x2, prm, batch, seq)
    return x2.reshape(batch, seq, D_MODEL).astype(x.dtype)
```

```python
import functools
import math

import jax
import jax.numpy as jnp
from jax import lax
from jax.experimental import pallas as pl
from jax.experimental.pallas import tpu as pltpu

F32 = jnp.float32
BF16 = jnp.bfloat16

D_MODEL = 1024
M_HEADS = 8
M_DQK = 64
M_DV = 128
M_QK = M_HEADS * M_DQK
M_V = M_HEADS * M_DV
M_CONV = 4
S_HEADDIM = 64
S_INNER = D_MODEL
S_HEADS = S_INNER // S_HEADDIM
S_GROUPS = 4
S_STATE = 128
S_CONV = 4
S_BC = S_GROUPS * S_STATE
S_XBC = S_INNER + 2 * S_BC
D_FF = 2752
FFN_CONV = 3
IN_SIZES = (M_QK, M_QK, M_V, M_V, M_HEADS, M_HEADS, S_INNER, S_XBC, S_HEADS, D_MODEL, D_MODEL)
LN_EPS = 1e-5
RMS_EPS = 1e-6
NEG_BIG = -1e30
LOG2E = math.log2(math.e)

LANES = 128
SUBLANES = 8
D_FF_PAD = 2816
TILE = 128
TILE_GROUPS = TILE // SUBLANES
FFN_COLS = 256
FFN_BLOCKS = D_FF_PAD // FFN_COLS
FFN_SUBTILES = 4
MIX_SUBTILES = 1
VMEM_LIMIT = 56 * 1024 * 1024
MIX_CONV_COLS = 2 * M_QK + S_XBC
MIX_CONV_BLOCK = 512
MIX_PREFIX = (M_CONV - 1) * SUBLANES
FFN_PREFIX = (FFN_CONV - 1) * SUBLANES

OFF_Q = 0
OFF_K = OFF_Q + M_QK
OFF_V = OFF_K + M_QK
OFF_O = OFF_V + M_V
OFF_Z = OFF_O + M_V
OFF_XBC = OFF_Z + S_INNER
OFF_GA = OFF_XBC + S_XBC
OFF_GB = OFF_GA + D_MODEL
BIG_COLS = OFF_GB + D_MODEL
LANE_I = 0
LANE_F = LANE_I + M_HEADS
LANE_DT = LANE_F + M_HEADS


def _layer_norm(y, g, b):
    mu = jnp.mean(y, axis=-1, keepdims=True)
    yc = y - mu
    var = jnp.mean(yc * yc, axis=-1, keepdims=True)
    return yc * lax.rsqrt(var + LN_EPS) * g + b


def _sigmoid(x):
    return 0.5 * jnp.tanh(0.5 * x) + 0.5


def _silu(x):
    h = 0.5 * x
    return h + h * jnp.tanh(h)


def _token_index(n):
    p = jnp.arange(n, dtype=jnp.int32)
    return (p & (SUBLANES - 1)) * TILE_GROUPS + (p >> 3)


def _rows_to_tile_order(a):
    return jnp.swapaxes(a.reshape(SUBLANES, TILE_GROUPS, a.shape[1]), 0, 1).reshape(a.shape)


def _rows_from_tile_order(a):
    return jnp.swapaxes(a.reshape(TILE_GROUPS, SUBLANES, a.shape[1]), 0, 1).reshape(a.shape)


def _causal_mask():
    tok = _token_index(TILE)
    return jnp.where(tok[:, None] >= tok[None, :], 0.0, -jnp.inf).astype(F32)


def _token_scan(a, combine, identity, seed=None):
    row8 = lax.broadcasted_iota(jnp.int32, (SUBLANES, a.shape[1]), 0)
    run, cur = [], None
    for j in range(TILE_GROUPS):
        grp = a[j * SUBLANES:(j + 1) * SUBLANES, :]
        cur = grp if cur is None else combine(cur, grp)
        run.append(cur)
    before = jnp.where(row8 >= 1, pltpu.roll(run[-1], 1, axis=0), identity)
    for sh in (1, 2, 4):
        before = combine(before, jnp.where(row8 >= sh, pltpu.roll(before, sh, axis=0), identity))
    if seed is not None:
        before = combine(before, seed)
    return jnp.concatenate([combine(r, before) for r in run], axis=0)


def _fill_conv_prefix(stage, carry, width, c0, c1):
    n_pre = width - 1
    row8 = lax.broadcasted_iota(jnp.int32, (SUBLANES, c1 - c0), 0)
    for g in range(n_pre):
        r0 = TILE + g * SUBLANES
        cur = stage[r0:r0 + SUBLANES, c0:c1]
        prev = carry[g * SUBLANES:(g + 1) * SUBLANES, c0:c1]
        stage[g * SUBLANES:(g + 1) * SUBLANES, c0:c1] = jnp.where(
            row8 == 0, pltpu.roll(prev, 1, axis=0), pltpu.roll(cur, 1, axis=0))
        carry[g * SUBLANES:(g + 1) * SUBLANES, c0:c1] = cur


def _conv_taps(stage, w_ref, b_ref, width, c0, c1):
    p = (width - 1) * SUBLANES
    acc = b_ref[:, c0:c1] + w_ref[width - 1:width, c0:c1] * stage[p:p + TILE, c0:c1]
    for k in range(1, width):
        r0 = p - k * SUBLANES
        acc = acc + w_ref[width - 1 - k:width - k, c0:c1] * stage[r0:r0 + TILE, c0:c1]
    return acc


def _mixer_kernel(alpha, entry, n_in, *refs):
    x_ref, params = refs[0], refs[1:n_in]
    o_ref, scratch = refs[n_in], refs[n_in + 1:]
    _, car_ref, mst_ref, mm_ref, sst_ref = scratch[:5]

    @pl.when(pl.program_id(1) == 0)
    def _():
        car_ref[...] = jnp.zeros_like(car_ref)
        mst_ref[...] = jnp.zeros_like(mst_ref)
        sst_ref[...] = jnp.zeros_like(sst_ref)
        mm_ref[...] = jnp.full_like(mm_ref, NEG_BIG * LOG2E)

    for sub in range(MIX_SUBTILES):
        rows = pl.ds(sub * TILE, TILE)
        _mixer_tile(alpha, entry, x_ref.at[rows], *params, o_ref.at[rows], *scratch)


def _mixer_tile(alpha, entry, x_ref, ing_ref, inb_ref, wb_ref, ws_ref, mask_ref, cw_ref, cb_ref, gbias_ref, aneg_ref,
                dskip_ref, mng_ref, sng_ref, pa_ref, pbw_ref, wo_ref, lng_ref, lnb_ref, o_ref,
                stage_ref, car_ref, mst_ref, mm_ref, sst_ref, pv_ref, pe_ref,
                qb_ref, kb_ref, kt_ref, xs_ref, bm_ref, cm_ref, bt_ref, ha_ref, hb_ref, ma_ref, xin_ref):
    L = TILE
    qk_cols = 2 * M_QK

    if entry:
        xin_ref[...] = _layer_norm(_rows_to_tile_order(x_ref[...]), ing_ref[...], inb_ref[...])

    def layer_input():
        return xin_ref[...] if entry else x_ref[...]

    xb = layer_input().astype(BF16)

    def proj(c0, width):
        return jnp.dot(xb, wb_ref[:, c0:c0 + width], preferred_element_type=F32)

    def proj_extra(piece):
        src = (OFF_O, OFF_Z, OFF_GA, OFF_GB)[piece // 4] + (piece % 4) * 256
        pe_ref[:, piece * 256:(piece + 1) * 256] = proj(src, 256)

    ps_raw = jnp.dot(xb, ws_ref[...], preferred_element_type=F32)
    stage_ref[MIX_PREFIX:, :qk_cols] = proj(OFF_Q, qk_cols)
    _fill_conv_prefix(stage_ref, car_ref, M_CONV, 0, qk_cols)
    stage_ref[MIX_PREFIX:, qk_cols:] = proj(OFF_XBC, S_XBC)

    ps = ps_raw + gbias_ref[...]
    lane = lax.broadcasted_iota(jnp.int32, (L, LANES), 1)
    tail = jnp.log1p(jnp.exp(-jnp.abs(ps)))
    log_sig = jnp.minimum(ps, 0.0) - tail
    soft_plus = jnp.maximum(ps, 0.0) + tail
    is_f = (lane >= LANE_F) & (lane < LANE_DT)
    to_cum = jnp.where(is_f, log_sig, soft_plus * aneg_ref[...])
    cum = _token_scan(to_cum, jnp.add, 0.0)
    packed = jnp.where(lane < LANE_F, ps, cum) * LOG2E
    packed_t = packed.T

    w_all = pltpu.roll(packed, LANE_F - LANE_I, axis=1) - packed
    m_prev_all = mm_ref[...]
    mm_all = _token_scan(w_all, jnp.maximum, -jnp.inf, seed=m_prev_all)
    s_inter_all = jnp.exp2(m_prev_all[0:1, :] - mm_all)
    floor_all = jnp.exp2(-(packed + mm_all))
    mm_last_all = mm_all[L - 1:L, :]
    mm_ref[...] = jnp.broadcast_to(packed[L - 1:L, :] + mm_last_all, (SUBLANES, LANES))

    def lane_rep(a, c):
        return jnp.broadcast_to(a[:, c:c + 1], (L, LANES))

    mm_rep = [lane_rep(mm_all, LANE_F + h) for h in range(M_HEADS)]
    s_inter_rep = [lane_rep(s_inter_all, LANE_F + h) for h in range(M_HEADS)]
    floor_rep = [lane_rep(floor_all, LANE_F + h) for h in range(M_HEADS)]
    acs_rep = [lane_rep(packed, LANE_DT + hh) for hh in range(S_HEADS)]
    dt_rep = [lane_rep(soft_plus, LANE_DT + hh) for hh in range(S_HEADS)]

    def conv_block(i):
        return _silu(_conv_taps(stage_ref, cw_ref, cb_ref, M_CONV, i * MIX_CONV_BLOCK, (i + 1) * MIX_CONV_BLOCK))

    pv_ref[...] = proj(OFF_V, M_V).astype(BF16)
    qb_ref[...] = conv_block(0).astype(BF16)
    proj_extra(0)
    k_all = conv_block(1) * (M_DQK ** -0.5)
    kb_ref[...] = k_all.astype(BF16)
    kt_ref[...] = k_all.T
    _fill_conv_prefix(stage_ref, car_ref, S_CONV, qk_cols, MIX_CONV_COLS)
    proj_extra(1)
    proj_extra(2)
    xs_ref[:, :MIX_CONV_BLOCK] = conv_block(2)
    proj_extra(3)
    proj_extra(4)
    xs_ref[:, MIX_CONV_BLOCK:] = conv_block(3)
    proj_extra(5)
    b_all = conv_block(4)
    bm_ref[...] = b_all.astype(BF16)
    bt_ref[...] = b_all.T.astype(BF16)
    proj_extra(6)
    proj_extra(7)
    cm_ref[...] = conv_block(5).astype(BF16)

    lane_pair = lax.broadcasted_iota(jnp.int32, (L, 2 * M_DQK), 1)
    ones_blk = jnp.ones((L, M_DV), BF16)
    state_pairs = [mst_ref[p] for p in range(M_HEADS // 2)]
    for h in range(M_HEADS):
        pair, half = divmod(h, 2)
        c = LANE_F + h
        w_row = packed_t[LANE_I + h:LANE_I + h + 1, :] - packed_t[c:c + 1, :]
        mm_2 = jnp.concatenate([mm_rep[h]] * (L // LANES), axis=1)
        decay = jnp.exp2((w_row + mask_ref[...]) - mm_2)
        in_head = (lane_pair >= half * M_DQK) & (lane_pair < (half + 1) * M_DQK)
        q_pair = qb_ref[:, pair * 2 * M_DQK:(pair + 1) * 2 * M_DQK]
        k_pair = kb_ref[:, pair * 2 * M_DQK:(pair + 1) * 2 * M_DQK]
        q_h = jnp.where(in_head, q_pair, jnp.zeros_like(q_pair))
        scores = lax.dot_general(q_h, k_pair, (((1,), (1,)), ((), ())), preferred_element_type=F32) * decay
        if h % 2 == 0:
            proj_extra(8 + h // 2)
        v_ext = jnp.concatenate([pv_ref[:, h * M_DV:(h + 1) * M_DV], ones_blk], axis=1)
        state_pair = state_pairs[pair]
        s_2 = jnp.concatenate([s_inter_rep[h], s_inter_rep[h]], axis=1)
        num_den = (jnp.dot(scores.astype(BF16), v_ext, preferred_element_type=F32)
                   + s_2 * jnp.dot(q_h, state_pair.astype(BF16), preferred_element_type=F32))
        kw_t = (kt_ref[h * M_DQK:(h + 1) * M_DQK, :]
                * jnp.exp2(w_row - mm_last_all[:, c:c + 1])).astype(BF16)
        mst_ref[pair, half * M_DQK:(half + 1) * M_DQK, :] = (
            s_inter_all[L - 1:L, c:c + 1] * state_pair[half * M_DQK:(half + 1) * M_DQK, :]
            + jnp.dot(kw_t, v_ext, preferred_element_type=F32))
        num = num_den[:, :M_DV]
        den = num_den[:, M_DV:]
        h_out = num / jnp.maximum(jnp.abs(den), floor_rep[h])
        ms = jnp.mean(h_out * h_out, axis=1, keepdims=True)
        h_n = h_out * lax.rsqrt(ms + RMS_EPS) * mng_ref[:, h * M_DV:(h + 1) * M_DV]
        o_gate = _sigmoid(pe_ref[:, h * M_DV:(h + 1) * M_DV])
        ha_ref[:, h * M_DV:(h + 1) * M_DV] = (o_gate * h_n).astype(BF16)

    ssd_states = [sst_ref[g] for g in range(S_GROUPS)]
    gw = S_INNER // S_GROUPS
    hpg = S_HEADS // S_GROUPS
    lane_g = lax.broadcasted_iota(jnp.int32, (L, gw), 1)
    low_half = lax.broadcasted_iota(jnp.int32, (L, LANES), 1) < S_HEADDIM
    assert hpg == 4 and 2 * S_HEADDIM == LANES

    for g in range(S_GROUPS):
        xs_g = xs_ref[:, g * gw:(g + 1) * gw]
        bm_g = bm_ref[:, g * S_STATE:(g + 1) * S_STATE]
        cm_g = cm_ref[:, g * S_STATE:(g + 1) * S_STATE]
        proj_extra(12 + g)
        ma_ref[:, g * gw:(g + 1) * gw] = jnp.dot(ha_ref[...], pa_ref[:, g * gw:(g + 1) * gw],
                                                 preferred_element_type=F32)
        cb = lax.dot_general(cm_g, bm_g, (((1,), (1,)), ((), ())), preferred_element_type=F32)
        y_off = jnp.dot(cm_g, ssd_states[g].astype(BF16), preferred_element_type=F32)
        h0 = g * hpg
        dt_g = jnp.concatenate([jnp.where(low_half, dt_rep[h0], dt_rep[h0 + 1]),
                                jnp.where(low_half, dt_rep[h0 + 2], dt_rep[h0 + 3])], axis=1)
        acs_g = jnp.concatenate([jnp.where(low_half, acs_rep[h0], acs_rep[h0 + 1]),
                                 jnp.where(low_half, acs_rep[h0 + 2], acs_rep[h0 + 3])], axis=1)
        last_g = acs_g[L - 1:L, :]
        xdt_g = xs_g * dt_g
        xdt_b = xdt_g.astype(BF16)
        y_diag = jnp.zeros((L, gw), F32)
        for r in range(hpg):
            c = LANE_DT + h0 + r
            acs_2 = jnp.concatenate([acs_rep[h0 + r]] * (L // LANES), axis=1)
            seg = jnp.exp2((acs_2 - packed_t[c:c + 1, :]) + mask_ref[...])
            mix = (cb * seg).astype(BF16)
            sel = (lane_g >= r * S_HEADDIM) & (lane_g < (r + 1) * S_HEADDIM)
            y_diag = y_diag + jnp.dot(mix, jnp.where(sel, xdt_b, jnp.zeros_like(xdt_b)),
                                      preferred_element_type=F32)
        y_g = y_diag + y_off * jnp.exp2(acs_g) + dskip_ref[:, g * gw:(g + 1) * gw] * xs_g
        x_dec = (xdt_g * jnp.exp2(last_g - acs_g)).astype(BF16)
        sst_ref[g] = (ssd_states[g] * jnp.exp2(last_g)
                      + jnp.dot(bt_ref[g * S_STATE:(g + 1) * S_STATE, :], x_dec, preferred_element_type=F32))
        gz = y_g * _silu(pe_ref[:, D_MODEL + g * gw:D_MODEL + (g + 1) * gw])
        ms = jnp.mean(gz * gz, axis=1, keepdims=True)
        hb_ref[:, g * gw:(g + 1) * gw] = (gz * lax.rsqrt(ms + RMS_EPS)
                                          * sng_ref[:, g * gw:(g + 1) * gw]).astype(BF16)

    gate_a = _sigmoid(pe_ref[:, 2 * D_MODEL:3 * D_MODEL])
    gate_b = _sigmoid(pe_ref[:, 3 * D_MODEL:])
    merged = gate_a * ma_ref[...] + gate_b * jnp.dot(hb_ref[...], pbw_ref[...], preferred_element_type=F32)
    y = alpha * layer_input() + jnp.dot(merged.astype(BF16), wo_ref[...], preferred_element_type=F32)
    o_ref[...] = _layer_norm(y, lng_ref[...], lnb_ref[...])


def _mixer(alpha, entry, x2, in_ln_g, in_ln_b, prm, batch, seq):
    L = TILE
    rows = MIX_SUBTILES * L
    nt = seq // rows
    row = lambda b, t: (b * nt + t, 0)
    const = lambda b, t: (0, 0)

    def full(a):
        return pl.BlockSpec(a.shape, const, pipeline_mode=pl.Buffered(1))

    args = (x2, in_ln_g, in_ln_b, prm["w_big"], prm["w_small"], _causal_mask(), prm["cw"], prm["cb"],
            prm["gbias"], prm["aneg"], prm["dskip"], prm["mng"], prm["sng"], prm["pa"], prm["pb"], prm["wo"],
            prm["ln1g"], prm["ln1b"])
    in_specs = [pl.BlockSpec((rows, D_MODEL), row)] + [full(a) for a in args[1:]]
    return pl.pallas_call(
        functools.partial(_mixer_kernel, alpha, entry, len(args)),
        grid=(batch, nt),
        in_specs=in_specs,
        out_specs=pl.BlockSpec((rows, D_MODEL), row),
        out_shape=jax.ShapeDtypeStruct((batch * seq, D_MODEL), F32),
        scratch_shapes=[
            pltpu.VMEM((MIX_PREFIX + L, MIX_CONV_COLS), F32),
            pltpu.VMEM((MIX_PREFIX, MIX_CONV_COLS), F32),
            pltpu.VMEM((M_HEADS // 2, 2 * M_DQK, 2 * M_DV), F32),
            pltpu.VMEM((M_HEADS, LANES), F32),
            pltpu.VMEM((S_GROUPS, S_STATE, S_INNER // S_GROUPS), F32),
            pltpu.VMEM((L, M_V), BF16),
            pltpu.VMEM((L, 4 * D_MODEL), F32),
            pltpu.VMEM((L, M_QK), BF16),
            pltpu.VMEM((L, M_QK), BF16),
            pltpu.VMEM((M_QK, L), F32),
            pltpu.VMEM((L, S_INNER), F32),
            pltpu.VMEM((L, S_BC), BF16),
            pltpu.VMEM((L, S_BC), BF16),
            pltpu.VMEM((S_BC, L), BF16),
            pltpu.VMEM((L, M_V), BF16),
            pltpu.VMEM((L, S_INNER), BF16),
            pltpu.VMEM((L, D_MODEL), F32),
            pltpu.VMEM((L, D_MODEL), F32),
        ],
        compiler_params=pltpu.CompilerParams(
            dimension_semantics=("parallel", "arbitrary"), vmem_limit_bytes=VMEM_LIMIT),
        name="mixer",
    )(*args)


def _ffn_kernel(alpha, exit_, x_ref, wg_ref, wv_ref, cwg_ref, cwv_ref, cbg_ref, cbv_ref, wd_ref, lng_ref, lnb_ref,
                o_ref, stage_ref, car_ref, acc_ref):
    bw = 2 * FFN_COLS

    @pl.when(pl.program_id(1) == 0)
    def _():
        car_ref[...] = jnp.zeros_like(car_ref)

    def tile_rows(sub):
        return slice(sub * TILE, (sub + 1) * TILE)

    def up_proj(xb, j):
        cols = slice(j * FFN_COLS, (j + 1) * FFN_COLS)
        stage_ref[j % 2, FFN_PREFIX:, :FFN_COLS] = jnp.dot(xb, wg_ref[:, cols], preferred_element_type=F32)
        stage_ref[j % 2, FFN_PREFIX:, FFN_COLS:] = jnp.dot(xb, wv_ref[:, cols], preferred_element_type=F32)

    xb = x_ref[tile_rows(0), :].astype(BF16)
    up_proj(xb, 0)
    for sub in range(FFN_SUBTILES):
        for j in range(FFN_BLOCKS):
            if j + 1 < FFN_BLOCKS:
                up_proj(xb, j + 1)
            stage = stage_ref.at[j % 2]
            cols = slice(j * FFN_COLS, (j + 1) * FFN_COLS)
            _fill_conv_prefix(stage, car_ref.at[j], FFN_CONV, 0, bw)
            conv_g = _conv_taps(stage.at[:, :FFN_COLS], cwg_ref.at[:, cols], cbg_ref.at[:, cols],
                                FFN_CONV, 0, FFN_COLS)
            conv_v = _conv_taps(stage.at[:, FFN_COLS:], cwv_ref.at[:, cols], cbv_ref.at[:, cols],
                                FFN_CONV, 0, FFN_COLS)
            act = (_silu(conv_g) * conv_v).astype(BF16)
            part = jnp.dot(act, wd_ref[cols, :], preferred_element_type=F32)
            if j == 0:
                acc_ref[sub] = part
            else:
                acc_ref[sub] += part
        if sub + 1 < FFN_SUBTILES:
            xb = x_ref[tile_rows(sub + 1), :].astype(BF16)
            up_proj(xb, 0)
        y = alpha * x_ref[tile_rows(sub), :] + acc_ref[sub]
        out = _layer_norm(y, lng_ref[...], lnb_ref[...])
        o_ref[tile_rows(sub), :] = _rows_from_tile_order(out) if exit_ else out


def _ffn(alpha, exit_, x2, prm, batch, seq):
    rows = FFN_SUBTILES * TILE
    nt = seq // rows
    row = lambda b, t: (b * nt + t, 0)
    const = lambda b, t: (0, 0)

    def full(a):
        return pl.BlockSpec(a.shape, const)

    args = (x2, prm["wg"], prm["wv"], prm["cwg"], prm["cwv"], prm["cbg"], prm["cbv"], prm["wd"],
            prm["ln2g"], prm["ln2b"])
    return pl.pallas_call(
        functools.partial(_ffn_kernel, alpha, exit_),
        grid=(batch, nt),
        in_specs=[pl.BlockSpec((rows, D_MODEL), row)] + [full(a) for a in args[1:]],
        out_specs=pl.BlockSpec((rows, D_MODEL), row),
        out_shape=jax.ShapeDtypeStruct((batch * seq, D_MODEL), F32),
        scratch_shapes=[pltpu.VMEM((2, FFN_PREFIX + TILE, 2 * FFN_COLS), F32),
                        pltpu.VMEM((FFN_BLOCKS, FFN_PREFIX, 2 * FFN_COLS), F32),
                        pltpu.VMEM((FFN_SUBTILES, TILE, D_MODEL), F32)],
        compiler_params=pltpu.CompilerParams(
            dimension_semantics=("parallel", "arbitrary"), vmem_limit_bytes=VMEM_LIMIT),
        name="conv_ffn",
    )(*args)


def _pad_ff(a, dtype):
    return jnp.pad(a.astype(dtype), ((0, 0), (0, D_FF_PAD - D_FF)))


def _layer_params(l, w_in, m_conv_w, m_conv_b, m_i_bias, m_f_bias, m_norm_g, s_conv_w, s_conv_b, s_dt_bias,
                  s_a_log, s_d, s_norm_g, p_a, p_b, w_out, ln1_g, ln1_b, w_up, f_conv_w, f_conv_b, w_down,
                  ln2_g, ln2_b):
    offs = [0]
    for s in IN_SIZES:
        offs.append(offs[-1] + s)
    w = w_in[l]
    part = [w[:, offs[i]:offs[i + 1]] for i in range(len(IN_SIZES))]
    wq, wk, wv, wo_, wi, wf, wz, wxbc, wdt, wga, wgb = part
    row = lambda a: a.reshape(1, -1).astype(F32)
    pad_small = LANES - 2 * M_HEADS - S_HEADS
    fcb = row(f_conv_b[l])
    prm = {
        "w_big": jnp.concatenate([wq, wk, wv, wo_, wz, wxbc, wga, wgb], axis=1).astype(BF16),
        "w_small": jnp.pad(jnp.concatenate([wi, wf, wdt], axis=1), ((0, 0), (0, pad_small))).astype(BF16),
        "cw": jnp.concatenate([m_conv_w[l], s_conv_w[l]], axis=1).astype(F32),
        "cb": jnp.concatenate([row(m_conv_b[l]), row(s_conv_b[l])], axis=1),
        "gbias": jnp.pad(jnp.concatenate([m_i_bias[l], m_f_bias[l], s_dt_bias[l]]).astype(F32),
                         (0, pad_small)).reshape(1, LANES),
        "aneg": jnp.pad(-jnp.exp(s_a_log[l].astype(F32)), (LANE_DT, pad_small)).reshape(1, LANES),
        "dskip": jnp.repeat(s_d[l].astype(F32), S_HEADDIM).reshape(1, S_INNER),
        "mng": row(m_norm_g[l]), "sng": row(s_norm_g[l]),
        "pa": p_a[l].astype(BF16), "pb": p_b[l].astype(BF16), "wo": w_out[l].astype(BF16),
        "ln1g": row(ln1_g[l]), "ln1b": row(ln1_b[l]),
        "wg": _pad_ff(w_up[l][:, :D_FF], BF16), "wv": _pad_ff(w_up[l][:, D_FF:], BF16),
        "cwg": _pad_ff(f_conv_w[l][:, :D_FF], F32), "cwv": _pad_ff(f_conv_w[l][:, D_FF:], F32),
        "cbg": _pad_ff(fcb[:, :D_FF], F32), "cbv": _pad_ff(fcb[:, D_FF:], F32),
        "wd": jnp.pad(w_down[l].astype(BF16), ((0, D_FF_PAD - D_FF), (0, 0))),
        "ln2g": row(ln2_g[l]), "ln2b": row(ln2_b[l]),
    }
    return prm


def kernel(x, in_ln_g, in_ln_b, w_in, m_conv_w, m_conv_b, m_i_bias, m_f_bias, m_norm_g, s_conv_w, s_conv_b, s_dt_bias, s_a_log, s_d, s_norm_g, p_a, p_b, w_out, ln1_g, ln1_b, w_up, f_conv_w, f_conv_b, w_down, ln2_g, ln2_b):
    batch, seq, d_model = x.shape
    depth = w_in.shape[0]
    assert d_model == D_MODEL and w_in.shape[2] == sum(IN_SIZES)
    assert seq % (FFN_SUBTILES * TILE) == 0 and seq % (MIX_SUBTILES * TILE) == 0 and depth >= 1
    alpha = float((2 * depth) ** 0.25)
    x2 = x.astype(F32).reshape(batch * seq, D_MODEL)
    in_g, in_b = in_ln_g.reshape(1, -1).astype(F32), in_ln_b.reshape(1, -1).astype(F32)
    for l in range(depth):
        prm = _layer_params(l, w_in, m_conv_w, m_conv_b, m_i_bias, m_f_bias, m_norm_g, s_conv_w, s_conv_b,
                            s_dt_bias, s_a_log, s_d, s_norm_g, p_a, p_b, w_out, ln1_g, ln1_b, w_up, f_conv_w,
                            f_conv_b, w_down, ln2_g, ln2_b)
        x2 = _mixer(alpha, l == 0, x2, in_g, in_b, prm, batch, seq)
        x2 = _ffn(alpha, l == depth - 1, x2, prm, batch, seq)
    return x2.reshape(batch, seq, D_MODEL).astype(x.dtype)
```
